```python
import math
import jax, jax.numpy as jnp
from jax import lax
import numpy as np

D_MODEL = 1024
BATCH = 16
SEQ = 2048
DEPTH = 1

CHUNK = 64
N_META = 16
Q_BLOCK = 128
ROPE_THETA = 10000.0
EPS = 1e-6
NEG_INF = -1e30
BIG_ID = 2 ** 30

DIFF_HEADS = 4
DIFF_HEAD_DIM = 64
DIFF_V_DIM = 2 * DIFF_HEAD_DIM
DIFF_QK_WIDTH = DIFF_HEADS * 2 * DIFF_HEAD_DIM
DIFF_WIDTH = DIFF_HEADS * DIFF_V_DIM

FOX_HEADS = 8
FOX_HEAD_DIM = 64
FOX_WIDTH = FOX_HEADS * FOX_HEAD_DIM

DQ_OFF = 0
DK_OFF = DQ_OFF + DIFF_QK_WIDTH
DV_OFF = DK_OFF + DIFF_QK_WIDTH
FQ_OFF = DV_OFF + DIFF_WIDTH
FK_OFF = FQ_OFF + FOX_WIDTH
FV_OFF = FK_OFF + FOX_WIDTH
FF_OFF = FV_OFF + FOX_WIDTH
GA_OFF = FF_OFF + FOX_HEADS
GB_OFF = GA_OFF + D_MODEL
IN_COLS = GB_OFF + D_MODEL

PEER_HEADS = 8
PEER_N_KEYS = 128
PEER_N_EXPERTS = PEER_N_KEYS * PEER_N_KEYS
PEER_KEY_DIM = 256
PEER_HALF = PEER_KEY_DIM // 2
PEER_TOPK = 16
PEER_TOKEN_BLOCK = 256

kernel_name = "hybrid_diff_fox_peer_block"


def _rmsnorm(x, g):
    xf = x.astype(jnp.float32)
    y = xf * lax.rsqrt(jnp.mean(xf * xf, axis=-1, keepdims=True) + EPS)
    return (y * g.astype(jnp.float32)).astype(x.dtype)


def _rope(x, T):
    d = x.shape[-1]
    inv_freq = ROPE_THETA ** (-jnp.arange(0, d, 2, dtype=jnp.float32) / d)
    ang = jnp.arange(T, dtype=jnp.float32)[:, None] * inv_freq[None, :]
    cos = jnp.concatenate([jnp.cos(ang), jnp.cos(ang)], axis=-1)
    sin = jnp.concatenate([jnp.sin(ang), jnp.sin(ang)], axis=-1)
    xf = x.astype(jnp.float32)
    x1, x2 = xf[..., : d // 2], xf[..., d // 2:]
    rot = jnp.concatenate([-x2, x1], axis=-1)
    return (xf * cos + rot * sin).astype(x.dtype)


def _blocked_attention(q, k, v, q_ids, k_ids, log_decay=None):
    B, G, H, T, d = q.shape
    dv = v.shape[-1]
    n_blk = T // Q_BLOCK
    scale = d ** -0.5

    def one_block(i):
        start = i * Q_BLOCK
        qb = lax.dynamic_slice_in_dim(q, start, Q_BLOCK, axis=3)
        s = jnp.einsum("bghqd,bghkd->bghqk", qb, k).astype(jnp.float32) * scale
        if log_decay is not None:
            cq = lax.dynamic_slice_in_dim(log_decay, start, Q_BLOCK, axis=2)
            s = s + (cq[:, None, :, :, None] - log_decay[:, None, :, None, :])
        qi = lax.dynamic_slice_in_dim(q_ids, start, Q_BLOCK, axis=0)
        allowed = k_ids[None, :] <= qi[:, None]
        s = jnp.where(allowed, s, NEG_INF)
        p = jax.nn.softmax(s, axis=-1).astype(v.dtype)
        return jnp.einsum("bghqk,bhkv->bghqv", p, v)

    out = lax.map(one_block, jnp.arange(n_blk))
    out = jnp.transpose(out, (1, 2, 3, 0, 4, 5))
    return out.reshape(B, G, H, T, dv)


def _token_mixer(h, layer, w_in, b_gate, b_forget, diff_qnorm_g, diff_knorm_g,
                 diff_lambda, diff_subln_g, fox_qnorm_g, fox_knorm_g,
                 w_branch_diff, w_branch_fox, w_out, chunk_ids, frame_ids):
    B, T, _ = h.shape
    proj = h @ w_in

    dq = proj[..., DQ_OFF:DK_OFF].reshape(B, T, 2, DIFF_HEADS, DIFF_HEAD_DIM)
    dk = proj[..., DK_OFF:DV_OFF].reshape(B, T, 2, DIFF_HEADS, DIFF_HEAD_DIM)
    dv = proj[..., DV_OFF:FQ_OFF].reshape(B, T, DIFF_HEADS, DIFF_V_DIM)
    dq = _rope(_rmsnorm(jnp.transpose(dq, (0, 2, 3, 1, 4)), diff_qnorm_g), T)
    dk = _rope(_rmsnorm(jnp.transpose(dk, (0, 2, 3, 1, 4)), diff_knorm_g), T)
    dv = jnp.transpose(dv, (0, 2, 1, 3))
    o = _blocked_attention(dq, dk, dv, chunk_ids, chunk_ids)
    lam_init = 0.8 - 0.6 * math.exp(-0.3 * layer)
    lam = diff_lambda.astype(jnp.float32)
    lam = (jnp.exp(jnp.sum(lam[0] * lam[1])) - jnp.exp(jnp.sum(lam[2] * lam[3]))
           + lam_init).astype(o.dtype)
    od = o[:, 0] - lam * o[:, 1]
    od = _rmsnorm(od, diff_subln_g) * (1.0 - lam_init)
    y_a = jnp.transpose(od, (0, 2, 1, 3)).reshape(B, T, DIFF_WIDTH)

    fq = proj[..., FQ_OFF:FK_OFF].reshape(B, T, FOX_HEADS, FOX_HEAD_DIM)
    fk = proj[..., FK_OFF:FV_OFF].reshape(B, T, FOX_HEADS, FOX_HEAD_DIM)
    fv = proj[..., FV_OFF:FF_OFF].reshape(B, T, FOX_HEADS, FOX_HEAD_DIM)
    fq = _rmsnorm(jnp.transpose(fq, (0, 2, 1, 3)), fox_qnorm_g)[:, None]
    fk = _rmsnorm(jnp.transpose(fk, (0, 2, 1, 3)), fox_knorm_g)[:, None]
    fv = jnp.transpose(fv, (0, 2, 1, 3))
    f_logit = (proj[..., FF_OFF:GA_OFF] + b_forget).astype(jnp.float32)
    log_f = jax.nn.log_sigmoid(f_logit)
    cum = jnp.transpose(jnp.cumsum(log_f, axis=1), (0, 2, 1))
    of = _blocked_attention(fq, fk, fv, frame_ids, frame_ids, log_decay=cum)[:, 0]
    y_b = jnp.transpose(of, (0, 2, 1, 3)).reshape(B, T, FOX_WIDTH)

    g_a = jax.nn.sigmoid(proj[..., GA_OFF:GB_OFF] + b_gate[:D_MODEL])
    g_b = jax.nn.sigmoid(proj[..., GB_OFF:IN_COLS] + b_gate[D_MODEL:])
    merged = g_a * (y_a @ w_branch_diff) + g_b * (y_b @ w_branch_fox)
    return merged @ w_out


def _peer(h, w_q, subkeys, u, v):
    B, T, D = h.shape
    n_tok = B * T
    n_pad = -(-n_tok // PEER_TOKEN_BLOCK) * PEER_TOKEN_BLOCK
    flat = jnp.pad(h.reshape(n_tok, D), ((0, n_pad - n_tok), (0, 0)))
    blocks = flat.reshape(n_pad // PEER_TOKEN_BLOCK, PEER_TOKEN_BLOCK, D)
    BLK = PEER_TOKEN_BLOCK

    def one_block(xb):
        q = (xb @ w_q).reshape(BLK, PEER_HEADS, 2, PEER_HALF)
        s = jnp.einsum("nhcd,hckd->nhck", q, subkeys).astype(jnp.float32)
        s1, i1 = lax.top_k(s[:, :, 0], PEER_TOPK)
        s2, i2 = lax.top_k(s[:, :, 1], PEER_TOPK)
        cand = (s1[..., :, None] + s2[..., None, :]).reshape(BLK, PEER_HEADS, PEER_TOPK * PEER_TOPK)
        cidx = (i1[..., :, None] * PEER_N_KEYS + i2[..., None, :]).reshape(BLK, PEER_HEADS, PEER_TOPK * PEER_TOPK)
        best, pos = lax.top_k(cand, PEER_TOPK)
        eidx = jnp.take_along_axis(cidx, pos, axis=-1).reshape(BLK, PEER_HEADS * PEER_TOPK)
        gate = jax.nn.softmax(best, axis=-1).reshape(BLK, PEER_HEADS * PEER_TOPK)
        ue = jnp.take(u, eidx, axis=0)
        act = jax.nn.gelu(jnp.einsum("nd,ned->ne", xb, ue), approximate=False)
        ve = jnp.take(v, eidx, axis=0)
        w = gate.astype(xb.dtype) * act
        return jnp.einsum("ne,ned->nd", w, ve)

    y = lax.map(one_block, blocks).reshape(n_pad, D)[:n_tok]
    return y.reshape(B, T, D)


def setup_inputs(seed: int = 0) -> dict:
    key = jax.random.key(seed)
    ks = jax.random.split(key, 20)
    f32 = jnp.float32
    nrm = lambda k, shape, s: jax.random.normal(k, shape, f32) * s
    gain = lambda k, shape: 1.0 + 0.05 * jax.random.normal(k, shape, f32)
    return {
        "x": jax.random.normal(ks[0], (BATCH, SEQ, D_MODEL), f32),
        "meta_tokens": nrm(ks[1], (N_META, D_MODEL), 1.0),
        "norm_mix_g": gain(ks[2], (DEPTH, D_MODEL)),
        "w_in": nrm(ks[3], (DEPTH, D_MODEL, IN_COLS), D_MODEL ** -0.5),
        "b_gate": nrm(ks[4], (DEPTH, 2 * D_MODEL), 0.02),
        "b_forget": 3.0 + nrm(ks[5], (DEPTH, FOX_HEADS), 0.1),
        "diff_qnorm_g": gain(ks[6], (DEPTH, DIFF_HEAD_DIM)),
        "diff_knorm_g": gain(ks[7], (DEPTH, DIFF_HEAD_DIM)),
        "diff_lambda": nrm(ks[8], (DEPTH, 4, DIFF_HEAD_DIM), 0.1),
        "diff_subln_g": gain(ks[9], (DEPTH, DIFF_V_DIM)),
        "fox_qnorm_g": gain(ks[10], (DEPTH, FOX_HEAD_DIM)),
        "fox_knorm_g": gain(ks[11], (DEPTH, FOX_HEAD_DIM)),
        "w_branch_diff": nrm(ks[12], (DEPTH, DIFF_WIDTH, D_MODEL), DIFF_WIDTH ** -0.5),
        "w_branch_fox": nrm(ks[13], (DEPTH, FOX_WIDTH, D_MODEL), FOX_WIDTH ** -0.5),
        "w_out": nrm(ks[14], (DEPTH, D_MODEL, D_MODEL), D_MODEL ** -0.5),
        "norm_ffn_g": gain(ks[15], (DEPTH, D_MODEL)),
        "peer_w_q": nrm(ks[16], (DEPTH, D_MODEL, PEER_HEADS * PEER_KEY_DIM), D_MODEL ** -0.5),
        "peer_subkeys": nrm(ks[17], (DEPTH, PEER_HEADS, 2, PEER_N_KEYS, PEER_HALF), PEER_HALF ** -0.5),
        "peer_u": nrm(ks[18], (DEPTH, PEER_N_EXPERTS, D_MODEL), D_MODEL ** -0.5),
        "peer_v": nrm(ks[19], (DEPTH, PEER_N_EXPERTS, D_MODEL), 0.5),
    }


def reference(x, meta_tokens, norm_mix_g, w_in, b_gate, b_forget, diff_qnorm_g,
              diff_knorm_g, diff_lambda, diff_subln_g, fox_qnorm_g, fox_knorm_g,
              w_branch_diff, w_branch_fox, w_out, norm_ffn_g, peer_w_q,
              peer_subkeys, peer_u, peer_v):
    B, S, D = x.shape
    T_ext = S + N_META
    T_pad = -(-T_ext // Q_BLOCK) * Q_BLOCK

    meta = jnp.broadcast_to(meta_tokens.astype(x.dtype)[None], (B, N_META, D))
    hstream = jnp.concatenate([meta, x], axis=1)

    pos = jnp.arange(T_pad, dtype=jnp.int32)
    chunk_ids = jnp.where(pos < N_META, 0, 1 + (pos - N_META) // CHUNK)
    chunk_ids = jnp.where(pos < T_ext, chunk_ids, BIG_ID).astype(jnp.int32)
    frame_ids = pos

    for layer in range(DEPTH):
        h = _rmsnorm(hstream, norm_mix_g[layer])
        h = jnp.pad(h, ((0, 0), (0, T_pad - T_ext), (0, 0)))
        mix = _token_mixer(h, layer, w_in[layer], b_gate[layer], b_forget[layer],
                           diff_qnorm_g[layer], diff_knorm_g[layer], diff_lambda[layer],
                           diff_subln_g[layer], fox_qnorm_g[layer], fox_knorm_g[layer],
                           w_branch_diff[layer], w_branch_fox[layer], w_out[layer],
                           chunk_ids, frame_ids)
        hstream = hstream + mix[:, :T_ext]
        h2 = _rmsnorm(hstream, norm_ffn_g[layer])
        hstream = hstream + _peer(h2, peer_w_q[layer], peer_subkeys[layer],
                                  peer_u[layer], peer_v[layer])

    return hstream[:, N_META:]
```

```python
import functools
import math

import jax
import jax.numpy as jnp
from jax import lax
from jax.experimental import pallas as pl
from jax.experimental.pallas import tpu as pltpu

F32 = jnp.float32
BF16 = jnp.bfloat16

D_MODEL = 1024
N_META = 16
META_ROWS = 128
CHUNK = 64
ROPE_THETA = 10000.0
EPS = 1e-6
NEG = -1e30

DIFF_HEADS = 4
HEAD_DIM = 64
FOX_HEADS = 8
BRANCH_WIDTH = 512

PEER_HEADS = 8
PEER_KEYS = 128
PEER_HALF = 128
PEER_TOPK = 16

_O_DQ, _O_DK, _O_DV, _O_FQ, _O_FK, _O_FV, _O_FF, _O_GA, _O_GB, _O_END = (
    0, 512, 1024, 1536, 2048, 2560, 3072, 3200, 4224, 5248)

VMEM_LIMIT_BYTES = 56 * 1024 * 1024


def _nt_dot(a, b):
    return lax.dot_general(a, b, (((1,), (1,)), ((), ())), preferred_element_type=F32)


def _largest_divisor(n, cap, mult):
    best = None
    for t in range(mult, min(n, cap) + 1, mult):
        if n % t == 0:
            best = t
    assert best is not None, (n, cap, mult)
    return best


def _group_mean_sq(y, gmat):
    v = y * y
    hi = v.astype(BF16)
    lo = (v - hi.astype(F32)).astype(BF16)
    return (jnp.dot(hi, gmat, preferred_element_type=F32)
            + jnp.dot(lo, gmat, preferred_element_type=F32)) * (1.0 / HEAD_DIM)


def _rope(y, cos, sin):
    lane = lax.broadcasted_iota(jnp.int32, y.shape, 1)
    first_half = (lane & (HEAD_DIM - 1)) < (HEAD_DIM // 2)
    w = y.shape[1]
    rot = jnp.where(first_half, -pltpu.roll(y, w - HEAD_DIM // 2, 1), pltpu.roll(y, HEAD_DIM // 2, 1))
    return y * cos + rot * sin


def _proj_kernel(x_ref, g_ref, w_ref, cos_ref, sin_ref, gq_ref, gk_ref, fgq_ref, fgk_ref, bf_ref, bg_ref,
                 gmat_ref, dq_ref, dk_ref, dv_ref, fq_ref, fk_ref, fv_ref, lf_ref, ga_ref, gb_ref,
                 *, tm, blocks_per_batch, n_valid):
    x = x_ref[...]
    h = (x * lax.rsqrt(jnp.mean(x * x, axis=-1, keepdims=True) + EPS) * g_ref[...]).astype(BF16)
    gmat = gmat_ref[...]
    cos = jnp.concatenate([cos_ref[...]] * 4, axis=1)
    sin = jnp.concatenate([sin_ref[...]] * 4, axis=1)

    def sec(lo, hi):
        return jnp.dot(h, w_ref[:, lo:hi], preferred_element_type=F32)

    def headnorm(y, gain):
        return y * lax.rsqrt(_group_mean_sq(y, gmat) + EPS) * gain

    scale = HEAD_DIM ** -0.5
    dq_ref[...] = (_rope(headnorm(sec(_O_DQ, _O_DK), gq_ref[...]), cos, sin) * scale).astype(BF16)
    dk_ref[...] = _rope(headnorm(sec(_O_DK, _O_DV), gk_ref[...]), cos, sin).astype(BF16)
    dv_ref[...] = sec(_O_DV, _O_FQ).astype(BF16)
    fq_ref[...] = (headnorm(sec(_O_FQ, _O_FK), fgq_ref[...]) * scale).astype(BF16)
    fk_ref[...] = headnorm(sec(_O_FK, _O_FV), fgk_ref[...]).astype(BF16)
    fv_ref[...] = sec(_O_FV, _O_FF).astype(BF16)

    z = sec(_O_FF, _O_GA) + bf_ref[...]
    log_f = jnp.minimum(z, 0.0) - jnp.log1p(jnp.exp(-jnp.abs(z)))
    row = (pl.program_id(0) % blocks_per_batch) * tm + lax.broadcasted_iota(jnp.int32, z.shape, 0)
    lf_ref[...] = jnp.where(row < n_valid, log_f, 0.0)

    ga_ref[...] = jax.nn.sigmoid(sec(_O_GA, _O_GB) + bg_ref[:, :D_MODEL]).astype(BF16)
    gb_ref[...] = jax.nn.sigmoid(sec(_O_GB, _O_END) + bg_ref[:, D_MODEL:]).astype(BF16)


def _proj(hin, g, w_all, cos, sin, gq, gk, fgq, fgk, bf, bg, gmat, *, t_int, n_valid):
    n = hin.shape[0]
    tm = _largest_divisor(t_int, 544, 16)
    bpb = t_int // tm
    row = lambda i: (i, 0)
    const = lambda i: (0, 0)
    tab = lambda i: (i % bpb, 0)
    wide = lambda w: pl.BlockSpec((tm, w), row)
    out_shapes = [jax.ShapeDtypeStruct((n, BRANCH_WIDTH), BF16)] * 6 + [
        jax.ShapeDtypeStruct((n, 128), F32),
        jax.ShapeDtypeStruct((n, D_MODEL), BF16), jax.ShapeDtypeStruct((n, D_MODEL), BF16)]
    return pl.pallas_call(
        functools.partial(_proj_kernel, tm=tm, blocks_per_batch=bpb, n_valid=n_valid),
        grid=(n // tm,),
        in_specs=[wide(D_MODEL), pl.BlockSpec((1, D_MODEL), const),
                  pl.BlockSpec((D_MODEL, _O_END), const, pipeline_mode=pl.Buffered(1)),
                  pl.BlockSpec((tm, 128), tab), pl.BlockSpec((tm, 128), tab),
                  pl.BlockSpec((1, BRANCH_WIDTH), const), pl.BlockSpec((1, BRANCH_WIDTH), const),
                  pl.BlockSpec((1, BRANCH_WIDTH), const), pl.BlockSpec((1, BRANCH_WIDTH), const),
                  pl.BlockSpec((1, 128), const), pl.BlockSpec((1, 2 * D_MODEL), const),
                  pl.BlockSpec((BRANCH_WIDTH, BRANCH_WIDTH), const)],
        out_specs=[wide(BRANCH_WIDTH)] * 6 + [wide(128), wide(D_MODEL), wide(D_MODEL)],
        out_shape=out_shapes,
        compiler_params=pltpu.CompilerParams(dimension_semantics=("arbitrary",),
                                             vmem_limit_bytes=VMEM_LIMIT_BYTES),
        name="proj",
    )(hin, g, w_all, cos, sin, gq, gk, fgq, fgk, bf, bg, gmat)


def _cum_kernel(lf_ref, tri_ref, col_ref, row_ref, *, n_blocks):
    tri = tri_ref[...]
    carry = jnp.zeros((1, 128), F32)
    for blk in [n_blocks - 1] + list(range(n_blocks - 1)):
        sl = slice(blk * 128, (blk + 1) * 128)
        v = lf_ref[0, sl, :]
        h1 = v.astype(BF16)
        r1 = v - h1.astype(F32)
        h2 = r1.astype(BF16)
        h3 = (r1 - h2.astype(F32)).astype(BF16)
        c = (jnp.dot(tri, h1, preferred_element_type=F32) + jnp.dot(tri, h2, preferred_element_type=F32)
             + jnp.dot(tri, h3, preferred_element_type=F32)) + carry
        col_ref[0, sl, :] = c
        row_ref[0, :, sl] = c.T[0:FOX_HEADS, :]
        carry = c[127:128, :]


def _cum(lf3, tri):
    b, t_int, _ = lf3.shape
    return pl.pallas_call(
        functools.partial(_cum_kernel, n_blocks=t_int // 128),
        grid=(b,),
        in_specs=[pl.BlockSpec((1, t_int, 128), lambda i: (i, 0, 0)), pl.BlockSpec((128, 128), lambda i: (0, 0))],
        out_specs=[pl.BlockSpec((1, t_int, 128), lambda i: (i, 0, 0)),
                   pl.BlockSpec((1, FOX_HEADS, t_int), lambda i: (i, 0, 0))],
        out_shape=[jax.ShapeDtypeStruct((b, t_int, 128), F32), jax.ShapeDtypeStruct((b, FOX_HEADS, t_int), F32)],
        compiler_params=pltpu.CompilerParams(dimension_semantics=("arbitrary",)),
        name="cum",
    )(lf3, tri)


def _online_softmax_step(qg, kblk, vblk, mask, bias, m, l, acc):
    s = _nt_dot(qg, kblk)
    if bias is not None:
        s = s + bias
    if mask is not None:
        s = jnp.where(mask, s, NEG)
    m_new = jnp.maximum(m, jnp.max(s, axis=1, keepdims=True))
    p = jnp.exp(s - m_new)
    alpha = jnp.exp(m - m_new)
    l = alpha * l + jnp.sum(p, axis=1, keepdims=True)
    acc = alpha * acc + jnp.dot(p.astype(BF16), vblk, preferred_element_type=F32)
    return m_new, l, acc


def _attn_init(tq):
    return (jnp.full((tq, 1), -jnp.inf, F32), jnp.zeros((tq, 1), F32), jnp.zeros((tq, 128), F32))


def _diff_attn_kernel(q_ref, k_ref, v_ref, lam_ref, sg_ref, o_ref, *, tq, s_real, lam_init):
    i = pl.program_id(2)
    q = q_ref[0]
    lane = lax.broadcasted_iota(jnp.int32, q.shape, 1)
    qs = (jnp.where(lane < HEAD_DIM, q, jnp.zeros_like(q)), jnp.where(lane >= HEAD_DIM, q, jnp.zeros_like(q)))

    def step(kblk, vblk, mask, carry):
        return tuple(_online_softmax_step(qs[g], kblk, vblk, mask, None, *carry[g]) for g in range(2))

    meta_mask = lax.broadcasted_iota(jnp.int32, (tq, META_ROWS), 1) < N_META
    carry = step(k_ref[0, s_real:s_real + META_ROWS, :], v_ref[0, s_real:s_real + META_ROWS, :], meta_mask,
                 (_attn_init(tq), _attn_init(tq)))

    def body(kc, carry):
        k0 = pl.multiple_of(kc * tq, tq)
        return step(k_ref[0, pl.ds(k0, tq), :], v_ref[0, pl.ds(k0, tq), :], None, carry)

    carry = lax.fori_loop(0, i, body, carry)

    k0 = pl.multiple_of(i * tq, tq)
    r = lax.broadcasted_iota(jnp.int32, (tq, tq), 0) // CHUNK
    c = lax.broadcasted_iota(jnp.int32, (tq, tq), 1) // CHUNK
    carry = step(k_ref[0, pl.ds(k0, tq), :], v_ref[0, pl.ds(k0, tq), :], c <= r, carry)

    (_, l0, a0), (_, l1, a1) = carry
    lam = lam_ref[...]
    lam = (jnp.exp(jnp.sum(lam[0:1] * lam[1:2], axis=1, keepdims=True))
           - jnp.exp(jnp.sum(lam[2:3] * lam[3:4], axis=1, keepdims=True)) + lam_init)
    od = a0 / l0 - lam * (a1 / l1)
    y = od * lax.rsqrt(jnp.mean(od * od, axis=-1, keepdims=True) + EPS) * sg_ref[...]
    o_ref[0] = (y * (1.0 - lam_init)).astype(BF16)


def _diff_attention(q, k, v, lam, sg, *, s_real, lam_init):
    b, t_int, _ = q.shape
    tq = min(256, s_real)
    return pl.pallas_call(
        functools.partial(_diff_attn_kernel, tq=tq, s_real=s_real, lam_init=lam_init),
        grid=(b, DIFF_HEADS, s_real // tq),
        in_specs=[pl.BlockSpec((1, tq, 128), lambda bi, h, i: (bi, i, h)),
                  pl.BlockSpec((1, t_int, 128), lambda bi, h, i: (bi, 0, h)),
                  pl.BlockSpec((1, t_int, 128), lambda bi, h, i: (bi, 0, h)),
                  pl.BlockSpec((4, HEAD_DIM), lambda bi, h, i: (0, 0)),
                  pl.BlockSpec((1, 128), lambda bi, h, i: (0, 0))],
        out_specs=pl.BlockSpec((1, tq, 128), lambda bi, h, i: (bi, i, h)),
        out_shape=jax.ShapeDtypeStruct((b, s_real, BRANCH_WIDTH), BF16),
        compiler_params=pltpu.CompilerParams(dimension_semantics=("arbitrary",) * 3),
        name="diff_attn",
    )(q, k, v, lam, sg)


def _fox_attn_kernel(q_ref, k_ref, v_ref, col_ref, row_ref, o_ref, *, tq, s_real):
    pair = pl.program_id(1)
    i = pl.program_id(2)
    q = q_ref[0]
    lane = lax.broadcasted_iota(jnp.int32, q.shape, 1)
    qs = (jnp.where(lane < HEAD_DIM, q, jnp.zeros_like(q)), jnp.where(lane >= HEAD_DIM, q, jnp.zeros_like(q)))
    cum = col_ref[0]
    heads = (2 * pair, 2 * pair + 1)
    cq = tuple(jnp.sum(jnp.where(lane == hd, cum, 0.0), axis=1, keepdims=True) for hd in heads)

    def step(k0, width, mask, carry):
        kblk = k_ref[0, pl.ds(k0, width), :]
        vblk = v_ref[0, pl.ds(k0, width), :]
        out = []
        for e in range(2):
            ck = row_ref[0, heads[e], :, pl.ds(k0, width)]
            out.append(_online_softmax_step(qs[e], kblk, vblk, mask, cq[e] - ck, *carry[e]))
        return tuple(out)

    meta_mask = lax.broadcasted_iota(jnp.int32, (tq, META_ROWS), 1) < N_META
    carry = step(s_real, META_ROWS, meta_mask, (_attn_init(tq), _attn_init(tq)))
    carry = lax.fori_loop(0, i, lambda kc, cr: step(pl.multiple_of(kc * tq, tq), tq, None, cr), carry)
    causal = lax.broadcasted_iota(jnp.int32, (tq, tq), 1) <= lax.broadcasted_iota(jnp.int32, (tq, tq), 0)
    carry = step(pl.multiple_of(i * tq, tq), tq, causal, carry)

    (_, l0, a0), (_, l1, a1) = carry
    o_ref[0] = jnp.where(lane < HEAD_DIM, a0 / l0, a1 / l1).astype(BF16)


def _fox_attention(q, k, v, cum_col, cum_row, *, s_real):
    b, t_int, _ = q.shape
    tq = min(256, s_real)
    return pl.pallas_call(
        functools.partial(_fox_attn_kernel, tq=tq, s_real=s_real),
        grid=(b, FOX_HEADS // 2, s_real // tq),
        in_specs=[pl.BlockSpec((1, tq, 128), lambda bi, p, i: (bi, i, p)),
                  pl.BlockSpec((1, t_int, 128), lambda bi, p, i: (bi, 0, p)),
                  pl.BlockSpec((1, t_int, 128), lambda bi, p, i: (bi, 0, p)),
                  pl.BlockSpec((1, tq, 128), lambda bi, p, i: (bi, i, 0)),
                  pl.BlockSpec((1, FOX_HEADS, 1, t_int), lambda bi, p, i: (bi, 0, 0, 0))],
        out_specs=pl.BlockSpec((1, tq, 128), lambda bi, p, i: (bi, i, p)),
        out_shape=jax.ShapeDtypeStruct((b, s_real, BRANCH_WIDTH), BF16),
        compiler_params=pltpu.CompilerParams(dimension_semantics=("arbitrary",) * 3),
        name="fox_attn",
    )(q, k, v, cum_col, cum_row)


def _merge_kernel(ya_ref, yb_ref, ga_ref, gb_ref, x_ref, wd_ref, wf_ref, wo_ref, g_ref, hs_ref, h2_ref):
    merged = (ga_ref[0].astype(F32) * jnp.dot(ya_ref[0], wd_ref[...], preferred_element_type=F32)
              + gb_ref[0].astype(F32) * jnp.dot(yb_ref[0], wf_ref[...], preferred_element_type=F32))
    hs = x_ref[0] + jnp.dot(merged.astype(BF16), wo_ref[...], preferred_element_type=F32)
    hs_ref[0] = hs
    h2_ref[0] = (hs * lax.rsqrt(jnp.mean(hs * hs, axis=-1, keepdims=True) + EPS) * g_ref[...]).astype(BF16)


def _merge(ya, yb, ga, gb, x, wd, wf, wo, g):
    b, s_real, _ = x.shape
    tm = min(512, s_real)
    blk = lambda w: pl.BlockSpec((1, tm, w), lambda bi, i: (bi, i, 0))
    const = lambda shape: pl.BlockSpec(shape, lambda bi, i: (0, 0))
    return pl.pallas_call(
        _merge_kernel,
        grid=(b, s_real // tm),
        in_specs=[blk(BRANCH_WIDTH), blk(BRANCH_WIDTH), blk(D_MODEL), blk(D_MODEL), blk(D_MODEL),
                  const((BRANCH_WIDTH, D_MODEL)), const((BRANCH_WIDTH, D_MODEL)), const((D_MODEL, D_MODEL)),
                  const((1, D_MODEL))],
        out_specs=[blk(D_MODEL), blk(D_MODEL)],
        out_shape=[jax.ShapeDtypeStruct((b, s_real, D_MODEL), F32), jax.ShapeDtypeStruct((b, s_real, D_MODEL), BF16)],
        compiler_params=pltpu.CompilerParams(dimension_semantics=("arbitrary",) * 2,
                                             vmem_limit_bytes=VMEM_LIMIT_BYTES),
        name="merge",
    )(ya, yb, ga, gb, x, wd, wf, wo, g)


def _route_kernel(h2_ref, wq_ref, sk_ref, e1_ref, jr_ref, e2_ref, rk2_ref, st_ref, val_ref, rank_ref):
    n_lists = 2 * PEER_HEADS
    tn = h2_ref.shape[0]
    q = jnp.dot(h2_ref[...], wq_ref[...], preferred_element_type=F32).astype(BF16)
    for li in range(n_lists):
        st_ref[li] = _nt_dot(sk_ref[li], q[:, li * PEER_HALF:(li + 1) * PEER_HALF])

    def list_body(li, _):
        s = st_ref[li]
        rows = lax.broadcasted_iota(jnp.int32, s.shape, 0).astype(F32)
        rank = jnp.full(s.shape, float(PEER_TOPK), F32)
        vals = []
        for it in range(PEER_TOPK):
            m = jnp.max(s, axis=0, keepdims=True)
            first = jnp.min(jnp.where(s == m, rows, float(PEER_KEYS)), axis=0, keepdims=True)
            hit = rows == first
            rank = jnp.where(hit, float(it), rank)
            s = jnp.where(hit, -jnp.inf, s)
            vals.append(m)
        val_ref[li] = jnp.concatenate(vals, axis=0)
        rank_ref[li] = rank
        return 0

    lax.fori_loop(0, n_lists, list_body, 0)

    n_cand = PEER_TOPK * PEER_TOPK

    def head_body(hd, _):
        a = val_ref[2 * hd]
        b = val_ref[2 * hd + 1]
        cand = jnp.concatenate([a[r:r + 1, :] + b for r in range(PEER_TOPK)], axis=0)
        pos = lax.broadcasted_iota(jnp.int32, cand.shape, 0).astype(F32)
        sel = jnp.zeros(cand.shape, F32)
        m0 = a[0:1, :] + b[0:1, :]
        z = jnp.zeros((1, tn), F32)
        for _it in range(PEER_TOPK):
            m = jnp.max(cand, axis=0, keepdims=True)
            first = jnp.min(jnp.where(cand == m, pos, float(n_cand)), axis=0, keepdims=True)
            hit = pos == first
            sel = jnp.where(hit, 1.0, sel)
            cand = jnp.where(hit, -jnp.inf, cand)
            z = z + jnp.exp(m - m0)
        rank1 = rank_ref[2 * hd]
        jr = jnp.zeros(rank1.shape, F32)
        for r in range(PEER_TOPK):
            j_r = jnp.sum(sel[r * PEER_TOPK:(r + 1) * PEER_TOPK, :], axis=0, keepdims=True)
            jr = jr + jnp.where(rank1 == float(r), j_r, 0.0)
        jr_ref[hd] = jr
        e1_ref[hd] = jnp.exp(st_ref[2 * hd] - a[0:1, :]) / z
        e2_ref[hd] = jnp.exp(st_ref[2 * hd + 1] - b[0:1, :]).astype(BF16)
        rk2_ref[hd] = rank_ref[2 * hd + 1].astype(BF16)
        return 0

    lax.fori_loop(0, PEER_HEADS, head_body, 0)


def _route(h2, wq, sk):
    n = h2.shape[0]
    tn = _largest_divisor(n, 512, 128)
    tok = lambda i: (0, 0, i)
    tab = lambda dt: jax.ShapeDtypeStruct((PEER_HEADS, PEER_KEYS, n), dt)
    return pl.pallas_call(
        _route_kernel,
        grid=(n // tn,),
        in_specs=[pl.BlockSpec((tn, D_MODEL), lambda i: (i, 0)),
                  pl.BlockSpec((D_MODEL, 2 * PEER_HEADS * PEER_HALF), lambda i: (0, 0)),
                  pl.BlockSpec((2 * PEER_HEADS, PEER_KEYS, PEER_HALF), lambda i: (0, 0, 0))],
        out_specs=[pl.BlockSpec((PEER_HEADS, PEER_KEYS, tn), tok)] * 4,
        out_shape=[tab(F32), tab(F32), tab(BF16), tab(BF16)],
        scratch_shapes=[pltpu.VMEM((2 * PEER_HEADS, PEER_KEYS, tn), F32),
                        pltpu.VMEM((2 * PEER_HEADS, PEER_TOPK, tn), F32),
                        pltpu.VMEM((2 * PEER_HEADS, PEER_KEYS, tn), F32)],
        compiler_params=pltpu.CompilerParams(dimension_semantics=("arbitrary",),
                                             vmem_limit_bytes=VMEM_LIMIT_BYTES),
        name="route",
    )(h2, wq, sk)


def _peer_kernel(h2_ref, u_ref, vt_ref, e1_ref, jr_ref, e2_ref, rk2_ref, hs_ref, o_ref, acc_ref, *, rows_per_step):
    c = pl.program_id(1)

    @pl.when(c == 0)
    def _():
        acc_ref[...] = jnp.zeros_like(acc_ref)

    s_t = _nt_dot(u_ref[...], h2_ref[...])
    act = 0.5 * s_t * (1.0 + lax.erf(s_t * (2.0 ** -0.5)))
    ws = []
    for rr in range(rows_per_step):
        gate = jnp.zeros((PEER_KEYS, s_t.shape[1]), F32)
        for hd in range(PEER_HEADS):
            j_r = jr_ref[hd, rr:rr + 1, :]
            e1 = e1_ref[hd, rr:rr + 1, :]
            gate = gate + jnp.where(rk2_ref[hd].astype(F32) < j_r, e2_ref[hd].astype(F32) * e1, 0.0)
        ws.append((act[rr * PEER_KEYS:(rr + 1) * PEER_KEYS, :] * gate).astype(BF16))
    w_t = jnp.concatenate(ws, axis=0)
    acc_ref[...] += jnp.dot(vt_ref[...], w_t, preferred_element_type=F32)

    @pl.when(c == pl.num_programs(1) - 1)
    def _():
        o_ref[...] = hs_ref[...] + acc_ref[...].T


def _peer(h2, u, vt, e1, jr, e2, rk2, hs):
    n = h2.shape[0]
    tn = _largest_divisor(n, 1024, 128)
    rows_per_step = 8
    ec = rows_per_step * PEER_KEYS
    tab = pl.BlockSpec((PEER_HEADS, PEER_KEYS, tn), lambda i, c: (0, 0, i))
    row_tab = pl.BlockSpec((PEER_HEADS, rows_per_step, tn), lambda i, c: (0, c, i))
    return pl.pallas_call(
        functools.partial(_peer_kernel, rows_per_step=rows_per_step),
        grid=(n // tn, PEER_KEYS // rows_per_step),
        in_specs=[pl.BlockSpec((tn, D_MODEL), lambda i, c: (i, 0)),
                  pl.BlockSpec((ec, D_MODEL), lambda i, c: (c, 0)),
                  pl.BlockSpec((D_MODEL, ec), lambda i, c: (0, c)),
                  row_tab, row_tab, tab, tab,
                  pl.BlockSpec((tn, D_MODEL), lambda i, c: (i, 0))],
        out_specs=pl.BlockSpec((tn, D_MODEL), lambda i, c: (i, 0)),
        out_shape=jax.ShapeDtypeStruct((n, D_MODEL), F32),
        scratch_shapes=[pltpu.VMEM((D_MODEL, tn), F32)],
        compiler_params=pltpu.CompilerParams(dimension_semantics=("arbitrary", "arbitrary"),
                                             vmem_limit_bytes=VMEM_LIMIT_BYTES),
        name="peer",
    )(h2, u, vt, e1, jr, e2, rk2, hs)


def _relayout_w_in(w):
    hq = lambda lo: (w[:, lo:lo + 512].reshape(D_MODEL, 2, DIFF_HEADS, HEAD_DIM)
                     .transpose(0, 2, 1, 3).reshape(D_MODEL, 512))
    ff = jnp.pad(w[:, 3072:3080], ((0, 0), (0, 120)))
    return jnp.concatenate([hq(0), hq(512), w[:, 1024:3072], ff, w[:, 3080:5128]], axis=1).astype(BF16)


def kernel(x, meta_tokens, norm_mix_g, w_in, b_gate, b_forget, diff_qnorm_g, diff_knorm_g, diff_lambda,
           diff_subln_g, fox_qnorm_g, fox_knorm_g, w_branch_diff, w_branch_fox, w_out, norm_ffn_g, peer_w_q,
           peer_subkeys, peer_u, peer_v):
    b, s_real, d = x.shape
    assert d == D_MODEL and s_real % 128 == 0 and norm_mix_g.shape[0] == 1
    t_int = s_real + META_ROWS
    layer = 0
    lam_init = 0.8 - 0.6 * math.exp(-0.3 * layer)

    meta_blk = jnp.pad(meta_tokens.astype(F32), ((0, META_ROWS - N_META), (0, 0)))
    hin = jnp.concatenate([x, jnp.broadcast_to(meta_blk[None], (b, META_ROWS, d))], axis=1).reshape(b * t_int, d)

    rows = jnp.arange(t_int, dtype=jnp.int32)
    pos = jnp.where(rows < s_real, rows + N_META, rows - s_real).astype(F32)
    inv_freq = ROPE_THETA ** (-jnp.arange(0, HEAD_DIM, 2, dtype=F32) / HEAD_DIM)
    ang = pos[:, None] * inv_freq[None, :]
    cos = jnp.tile(jnp.cos(ang), (1, 4))
    sin = jnp.tile(jnp.sin(ang), (1, 4))

    tile8 = lambda g: jnp.tile(g.astype(F32), 8)[None, :]
    gidx = jnp.arange(BRANCH_WIDTH) // HEAD_DIM
    gmat = (gidx[:, None] == gidx[None, :]).astype(BF16)
    bf = jnp.pad(b_forget[layer].astype(F32), (0, 120))[None, :]

    dq, dk, dv, fq, fk, fv, lf, ga, gb = _proj(
        hin, norm_mix_g[layer][None, :], _relayout_w_in(w_in[layer]), cos, sin,
        tile8(diff_qnorm_g[layer]), tile8(diff_knorm_g[layer]), tile8(fox_qnorm_g[layer]), tile8(fox_knorm_g[layer]),
        bf, b_gate[layer][None, :], gmat, t_int=t_int, n_valid=s_real + N_META)

    r3 = lambda a: a.reshape(b, t_int, a.shape[-1])
    tri = (jnp.arange(128)[:, None] >= jnp.arange(128)[None, :]).astype(BF16)
    cum_col, cum_row = _cum(r3(lf), tri)

    ya = _diff_attention(r3(dq), r3(dk), r3(dv), diff_lambda[layer].astype(F32),
                         diff_subln_g[layer].astype(F32)[None, :], s_real=s_real, lam_init=lam_init)
    yb = _fox_attention(r3(fq), r3(fk), r3(fv), cum_col, cum_row[:, :, None, :], s_real=s_real)

    hs, h2 = _merge(ya, yb, r3(ga), r3(gb), x, w_branch_diff[layer].astype(BF16), w_branch_fox[layer].astype(BF16),
                    w_out[layer].astype(BF16), norm_ffn_g[layer][None, :])

    n = b * s_real
    h2 = h2.reshape(n, d)
    sk = peer_subkeys[layer].astype(BF16).reshape(2 * PEER_HEADS, PEER_KEYS, PEER_HALF)
    e1, jr, e2, rk2 = _route(h2, peer_w_q[layer].astype(BF16), sk)
    out = _peer(h2, peer_u[layer].astype(BF16), peer_v[layer].astype(BF16).T, e1, jr, e2, rk2, hs.reshape(n, d))
    return out.reshape(b, s_real, d)
```

```python
import functools
import math

import jax
import jax.numpy as jnp
from jax import lax
from jax.experimental import pallas as pl
from jax.experimental.pallas import tpu as pltpu

F32 = jnp.float32
BF16 = jnp.bfloat16

D_MODEL = 1024
N_META = 16
META_ROWS = 128
CHUNK = 64
ROPE_THETA = 10000.0
EPS = 1e-6
NEG = -1e30

DIFF_HEADS = 4
HEAD_DIM = 64
FOX_HEADS = 8
BRANCH_WIDTH = 512

PEER_HEADS = 8
PEER_KEYS = 128
PEER_HALF = 128
PEER_TOPK = 16

_O_DQ, _O_DK, _O_DV, _O_FQ, _O_FK, _O_FV, _O_FF, _O_GA, _O_GB, _O_END = (
    0, 512, 1024, 1536, 2048, 2560, 3072, 3200, 4224, 5248)

VMEM_LIMIT_BYTES = 56 * 1024 * 1024


def _nt_dot(a, b):
    return lax.dot_general(a, b, (((1,), (1,)), ((), ())), preferred_element_type=F32)


def _largest_divisor(n, cap, mult):
    best = None
    for t in range(mult, min(n, cap) + 1, mult):
        if n % t == 0:
            best = t
    assert best is not None, (n, cap, mult)
    return best


def _group_mean_sq(y, gmat):
    v = y * y
    hi = v.astype(BF16)
    lo = (v - hi.astype(F32)).astype(BF16)
    return (jnp.dot(hi, gmat, preferred_element_type=F32)
            + jnp.dot(lo, gmat, preferred_element_type=F32)) * (1.0 / HEAD_DIM)


def _rope(y, cos, sin):
    lane = lax.broadcasted_iota(jnp.int32, y.shape, 1)
    first_half = (lane & (HEAD_DIM - 1)) < (HEAD_DIM // 2)
    w = y.shape[1]
    rot = jnp.where(first_half, -pltpu.roll(y, w - HEAD_DIM // 2, 1), pltpu.roll(y, HEAD_DIM // 2, 1))
    return y * cos + rot * sin


def _proj_kernel(x_ref, g_ref, w_ref, cos_ref, sin_ref, gq_ref, gk_ref, fgq_ref, fgk_ref, bf_ref, bg_ref,
                 gmat_ref, dq_ref, dk_ref, dv_ref, fq_ref, fk_ref, fv_ref, lf_ref, ga_ref, gb_ref,
                 *, tm, blocks_per_batch, n_valid):
    x = x_ref[...]
    h = (x * lax.rsqrt(jnp.mean(x * x, axis=-1, keepdims=True) + EPS) * g_ref[...]).astype(BF16)
    gmat = gmat_ref[...]
    cos = jnp.concatenate([cos_ref[...]] * 4, axis=1)
    sin = jnp.concatenate([sin_ref[...]] * 4, axis=1)

    def sec(lo, hi):
        return jnp.dot(h, w_ref[:, lo:hi], preferred_element_type=F32)

    def headnorm(y, gain):
        return y * lax.rsqrt(_group_mean_sq(y, gmat) + EPS) * gain

    scale = HEAD_DIM ** -0.5
    dq_ref[...] = (_rope(headnorm(sec(_O_DQ, _O_DK), gq_ref[...]), cos, sin) * scale).astype(BF16)
    dk_ref[...] = _rope(headnorm(sec(_O_DK, _O_DV), gk_ref[...]), cos, sin).astype(BF16)
    dv_ref[...] = sec(_O_DV, _O_FQ).astype(BF16)
    fq_ref[...] = (headnorm(sec(_O_FQ, _O_FK), fgq_ref[...]) * scale).astype(BF16)
    fk_ref[...] = headnorm(sec(_O_FK, _O_FV), fgk_ref[...]).astype(BF16)
    fv_ref[...] = sec(_O_FV, _O_FF).astype(BF16)

    z = sec(_O_FF, _O_GA) + bf_ref[...]
    log_f = jnp.minimum(z, 0.0) - jnp.log1p(jnp.exp(-jnp.abs(z)))
    row = (pl.program_id(0) % blocks_per_batch) * tm + lax.broadcasted_iota(jnp.int32, z.shape, 0)
    lf_ref[...] = jnp.where(row < n_valid, log_f, 0.0)

    ga_ref[...] = jax.nn.sigmoid(sec(_O_GA, _O_GB) + bg_ref[:, :D_MODEL]).astype(BF16)
    gb_ref[...] = jax.nn.sigmoid(sec(_O_GB, _O_END) + bg_ref[:, D_MODEL:]).astype(BF16)


def _proj(hin, g, w_all, cos, sin, gq, gk, fgq, fgk, bf, bg, gmat, *, t_int, n_valid):
    n = hin.shape[0]
    tm = _largest_divisor(t_int, 544, 16)
    bpb = t_int // tm
    row = lambda i: (i, 0)
    const = lambda i: (0, 0)
    tab = lambda i: (i % bpb, 0)
    wide = lambda w: pl.BlockSpec((tm, w), row)
    out_shapes = [jax.ShapeDtypeStruct((n, BRANCH_WIDTH), BF16)] * 6 + [
        jax.ShapeDtypeStruct((n, 128), F32),
        jax.ShapeDtypeStruct((n, D_MODEL), BF16), jax.ShapeDtypeStruct((n, D_MODEL), BF16)]
    return pl.pallas_call(
        functools.partial(_proj_kernel, tm=tm, blocks_per_batch=bpb, n_valid=n_valid),
        grid=(n // tm,),
        in_specs=[wide(D_MODEL), pl.BlockSpec((1, D_MODEL), const),
                  pl.BlockSpec((D_MODEL, _O_END), const, pipeline_mode=pl.Buffered(1)),
                  pl.BlockSpec((tm, 128), tab), pl.BlockSpec((tm, 128), tab),
                  pl.BlockSpec((1, BRANCH_WIDTH), const), pl.BlockSpec((1, BRANCH_WIDTH), const),
                  pl.BlockSpec((1, BRANCH_WIDTH), const), pl.BlockSpec((1, BRANCH_WIDTH), const),
                  pl.BlockSpec((1, 128), const), pl.BlockSpec((1, 2 * D_MODEL), const),
                  pl.BlockSpec((BRANCH_WIDTH, BRANCH_WIDTH), const)],
        out_specs=[wide(BRANCH_WIDTH)] * 6 + [wide(128), wide(D_MODEL), wide(D_MODEL)],
        out_shape=out_shapes,
        compiler_params=pltpu.CompilerParams(dimension_semantics=("arbitrary",),
                                             vmem_limit_bytes=VMEM_LIMIT_BYTES),
        name="proj",
    )(hin, g, w_all, cos, sin, gq, gk, fgq, fgk, bf, bg, gmat)


def _cum_kernel(lf_ref, tri_ref, col_ref, row_ref, *, n_blocks):
    tri = tri_ref[...]
    carry = jnp.zeros((1, 128), F32)
    for blk in [n_blocks - 1] + list(range(n_blocks - 1)):
        sl = slice(blk * 128, (blk + 1) * 128)
        v = lf_ref[0, sl, :]
        h1 = v.astype(BF16)
        r1 = v - h1.astype(F32)
        h2 = r1.astype(BF16)
        h3 = (r1 - h2.astype(F32)).astype(BF16)
        c = (jnp.dot(tri, h1, preferred_element_type=F32) + jnp.dot(tri, h2, preferred_element_type=F32)
             + jnp.dot(tri, h3, preferred_element_type=F32)) + carry
        col_ref[0, sl, :] = c
        row_ref[0, :, sl] = c.T[0:FOX_HEADS, :]
        carry = c[127:128, :]


def _cum(lf3, tri):
    b, t_int, _ = lf3.shape
    return pl.pallas_call(
        functools.partial(_cum_kernel, n_blocks=t_int // 128),
        grid=(b,),
        in_specs=[pl.BlockSpec((1, t_int, 128), lambda i: (i, 0, 0)), pl.BlockSpec((128, 128), lambda i: (0, 0))],
        out_specs=[pl.BlockSpec((1, t_int, 128), lambda i: (i, 0, 0)),
                   pl.BlockSpec((1, FOX_HEADS, t_int), lambda i: (i, 0, 0))],
        out_shape=[jax.ShapeDtypeStruct((b, t_int, 128), F32), jax.ShapeDtypeStruct((b, FOX_HEADS, t_int), F32)],
        compiler_params=pltpu.CompilerParams(dimension_semantics=("arbitrary",)),
        name="cum",
    )(lf3, tri)


def _online_softmax_step(qg, kblk, vblk, mask, bias, m, l, acc):
    s = _nt_dot(qg, kblk)
    if bias is not None:
        s = s + bias
    if mask is not None:
        s = jnp.where(mask, s, NEG)
    m_new = jnp.maximum(m, jnp.max(s, axis=1, keepdims=True))
    p = jnp.exp(s - m_new)
    alpha = jnp.exp(m - m_new)
    l = alpha * l + jnp.sum(p, axis=1, keepdims=True)
    acc = alpha * acc + jnp.dot(p.astype(BF16), vblk, preferred_element_type=F32)
    return m_new, l, acc


def _attn_init(tq):
    return (jnp.full((tq, 1), -jnp.inf, F32), jnp.zeros((tq, 1), F32), jnp.zeros((tq, 128), F32))


def _diff_attn_kernel(q_ref, k_ref, v_ref, lam_ref, sg_ref, o_ref, *, tq, s_real, lam_init):
    i = pl.program_id(2)
    q = q_ref[0]
    lane = lax.broadcasted_iota(jnp.int32, q.shape, 1)
    qs = (jnp.where(lane < HEAD_DIM, q, jnp.zeros_like(q)), jnp.where(lane >= HEAD_DIM, q, jnp.zeros_like(q)))

    def step(kblk, vblk, mask, carry):
        return tuple(_online_softmax_step(qs[g], kblk, vblk, mask, None, *carry[g]) for g in range(2))

    meta_mask = lax.broadcasted_iota(jnp.int32, (tq, META_ROWS), 1) < N_META
    carry = step(k_ref[0, s_real:s_real + META_ROWS, :], v_ref[0, s_real:s_real + META_ROWS, :], meta_mask,
                 (_attn_init(tq), _attn_init(tq)))

    def body(kc, carry):
        k0 = pl.multiple_of(kc * tq, tq)
        return step(k_ref[0, pl.ds(k0, tq), :], v_ref[0, pl.ds(k0, tq), :], None, carry)

    carry = lax.fori_loop(0, i, body, carry)

    k0 = pl.multiple_of(i * tq, tq)
    r = lax.broadcasted_iota(jnp.int32, (tq, tq), 0) // CHUNK
    c = lax.broadcasted_iota(jnp.int32, (tq, tq), 1) // CHUNK
    carry = step(k_ref[0, pl.ds(k0, tq), :], v_ref[0, pl.ds(k0, tq), :], c <= r, carry)

    (_, l0, a0), (_, l1, a1) = carry
    lam = lam_ref[...]
    lam = (jnp.exp(jnp.sum(lam[0:1] * lam[1:2], axis=1, keepdims=True))
           - jnp.exp(jnp.sum(lam[2:3] * lam[3:4], axis=1, keepdims=True)) + lam_init)
    od = a0 / l0 - lam * (a1 / l1)
    y = od * lax.rsqrt(jnp.mean(od * od, axis=-1, keepdims=True) + EPS) * sg_ref[...]
    o_ref[0] = (y * (1.0 - lam_init)).astype(BF16)


def _diff_attention(q, k, v, lam, sg, *, s_real, lam_init):
    b, t_int, _ = q.shape
    tq = min(256, s_real)
    return pl.pallas_call(
        functools.partial(_diff_attn_kernel, tq=tq, s_real=s_real, lam_init=lam_init),
        grid=(b, DIFF_HEADS, s_real // tq),
        in_specs=[pl.BlockSpec((1, tq, 128), lambda bi, h, i: (bi, i, h)),
                  pl.BlockSpec((1, t_int, 128), lambda bi, h, i: (bi, 0, h)),
                  pl.BlockSpec((1, t_int, 128), lambda bi, h, i: (bi, 0, h)),
                  pl.BlockSpec((4, HEAD_DIM), lambda bi, h, i: (0, 0)),
                  pl.BlockSpec((1, 128), lambda bi, h, i: (0, 0))],
        out_specs=pl.BlockSpec((1, tq, 128), lambda bi, h, i: (bi, i, h)),
        out_shape=jax.ShapeDtypeStruct((b, s_real, BRANCH_WIDTH), BF16),
        compiler_params=pltpu.CompilerParams(dimension_semantics=("arbitrary",) * 3),
        name="diff_attn",
    )(q, k, v, lam, sg)


def _fox_attn_kernel(q_ref, k_ref, v_ref, col_ref, row_ref, o_ref, *, tq, s_real):
    pair = pl.program_id(1)
    i = pl.program_id(2)
    q = q_ref[0]
    lane = lax.broadcasted_iota(jnp.int32, q.shape, 1)
    qs = (jnp.where(lane < HEAD_DIM, q, jnp.zeros_like(q)), jnp.where(lane >= HEAD_DIM, q, jnp.zeros_like(q)))
    cum = col_ref[0]
    heads = (2 * pair, 2 * pair + 1)
    cq = tuple(jnp.sum(jnp.where(lane == hd, cum, 0.0), axis=1, keepdims=True) for hd in heads)

    def step(k0, width, mask, carry):
        kblk = k_ref[0, pl.ds(k0, width), :]
        vblk = v_ref[0, pl.ds(k0, width), :]
        out = []
        for e in range(2):
            ck = row_ref[0, heads[e], :, pl.ds(k0, width)]
            out.append(_online_softmax_step(qs[e], kblk, vblk, mask, cq[e] - ck, *carry[e]))
        return tuple(out)

    meta_mask = lax.broadcasted_iota(jnp.int32, (tq, META_ROWS), 1) < N_META
    carry = step(s_real, META_ROWS, meta_mask, (_attn_init(tq), _attn_init(tq)))
    carry = lax.fori_loop(0, i, lambda kc, cr: step(pl.multiple_of(kc * tq, tq), tq, None, cr), carry)
    causal = lax.broadcasted_iota(jnp.int32, (tq, tq), 1) <= lax.broadcasted_iota(jnp.int32, (tq, tq), 0)
    carry = step(pl.multiple_of(i * tq, tq), tq, causal, carry)

    (_, l0, a0), (_, l1, a1) = carry
    o_ref[0] = jnp.where(lane < HEAD_DIM, a0 / l0, a1 / l1).astype(BF16)


def _fox_attention(q, k, v, cum_col, cum_row, *, s_real):
    b, t_int, _ = q.shape
    tq = min(256, s_real)
    return pl.pallas_call(
        functools.partial(_fox_attn_kernel, tq=tq, s_real=s_real),
        grid=(b, FOX_HEADS // 2, s_real // tq),
        in_specs=[pl.BlockSpec((1, tq, 128), lambda bi, p, i: (bi, i, p)),
                  pl.BlockSpec((1, t_int, 128), lambda bi, p, i: (bi, 0, p)),
                  pl.BlockSpec((1, t_int, 128), lambda bi, p, i: (bi, 0, p)),
                  pl.BlockSpec((1, tq, 128), lambda bi, p, i: (bi, i, 0)),
                  pl.BlockSpec((1, FOX_HEADS, 1, t_int), lambda bi, p, i: (bi, 0, 0, 0))],
        out_specs=pl.BlockSpec((1, tq, 128), lambda bi, p, i: (bi, i, p)),
        out_shape=jax.ShapeDtypeStruct((b, s_real, BRANCH_WIDTH), BF16),
        compiler_params=pltpu.CompilerParams(dimension_semantics=("arbitrary",) * 3),
        name="fox_attn",
    )(q, k, v, cum_col, cum_row)


def _merge_kernel(ya_ref, yb_ref, ga_ref, gb_ref, x_ref, wd_ref, wf_ref, wo_ref, g_ref, hs_ref, h2_ref):
    merged = (ga_ref[0].astype(F32) * jnp.dot(ya_ref[0], wd_ref[...], preferred_element_type=F32)
              + gb_ref[0].astype(F32) * jnp.dot(yb_ref[0], wf_ref[...], preferred_element_type=F32))
    hs = x_ref[0] + jnp.dot(merged.astype(BF16), wo_ref[...], preferred_element_type=F32)
    hs_ref[0] = hs
    h2_ref[0] = (hs * lax.rsqrt(jnp.mean(hs * hs, axis=-1, keepdims=True) + EPS) * g_ref[...]).astype(BF16)


def _merge(ya, yb, ga, gb, x, wd, wf, wo, g):
    b, s_real, _ = x.shape
    tm = min(512, s_real)
    blk = lambda w: pl.BlockSpec((1, tm, w), lambda bi, i: (bi, i, 0))
    const = lambda shape: pl.BlockSpec(shape, lambda bi, i: (0, 0))
    return pl.pallas_call(
        _merge_kernel,
        grid=(b, s_real // tm),
        in_specs=[blk(BRANCH_WIDTH), blk(BRANCH_WIDTH), blk(D_MODEL), blk(D_MODEL), blk(D_MODEL),
                  const((BRANCH_WIDTH, D_MODEL)), const((BRANCH_WIDTH, D_MODEL)), const((D_MODEL, D_MODEL)),
                  const((1, D_MODEL))],
        out_specs=[blk(D_MODEL), blk(D_MODEL)],
        out_shape=[jax.ShapeDtypeStruct((b, s_real, D_MODEL), F32), jax.ShapeDtypeStruct((b, s_real, D_MODEL), BF16)],
        compiler_params=pltpu.CompilerParams(dimension_semantics=("arbitrary",) * 2,
                                             vmem_limit_bytes=VMEM_LIMIT_BYTES),
        name="merge",
    )(ya, yb, ga, gb, x, wd, wf, wo, g)


def _route_kernel(h2_ref, wq_ref, sk_ref, e1_ref, jr_ref, e2_ref, rk2_ref, st_ref, val_ref, rank_ref):
    n_lists = 2 * PEER_HEADS
    tn = h2_ref.shape[0]
    q = jnp.dot(h2_ref[...], wq_ref[...], preferred_element_type=F32).astype(BF16)
    for li in range(n_lists):
        st_ref[li] = _nt_dot(sk_ref[li], q[:, li * PEER_HALF:(li + 1) * PEER_HALF])

    def list_body(li, _):
        s = st_ref[li]
        rows = lax.broadcasted_iota(jnp.int32, s.shape, 0).astype(F32)
        rank = jnp.full(s.shape, float(PEER_TOPK), F32)
        vals = []
        for it in range(PEER_TOPK):
            m = jnp.max(s, axis=0, keepdims=True)
            first = jnp.min(jnp.where(s == m, rows, float(PEER_KEYS)), axis=0, keepdims=True)
            hit = rows == first
            rank = jnp.where(hit, float(it), rank)
            s = jnp.where(hit, -jnp.inf, s)
            vals.append(m)
        val_ref[li] = jnp.concatenate(vals, axis=0)
        rank_ref[li] = rank
        return 0

    lax.fori_loop(0, n_lists, list_body, 0)

    n_cand = PEER_TOPK * PEER_TOPK
    half = PEER_TOPK // 2
    iota8 = lax.broadcasted_iota(jnp.int32, (half, tn), 0)
    pos = jnp.concatenate(
        [lax.broadcasted_iota(jnp.int32, (PEER_TOPK, tn), 0)]
        + [iota8 + PEER_TOPK * r for r in range(1, half)]
        + [(iota8 + half) * PEER_TOPK], axis=0).astype(F32)

    def head_body(hd, _):
        a = val_ref[2 * hd]
        b = val_ref[2 * hd + 1]
        cand = jnp.concatenate(
            [a[0:1, :] + b] + [a[r:r + 1, :] + b[0:half, :] for r in range(1, half)]
            + [a[half:PEER_TOPK, :] + b[0:1, :]], axis=0)
        sel = jnp.zeros(cand.shape, F32)
        m0 = a[0:1, :] + b[0:1, :]
        z = jnp.zeros((1, tn), F32)
        for _it in range(PEER_TOPK):
            m = jnp.max(cand, axis=0, keepdims=True)
            first = jnp.min(jnp.where(cand == m, pos, float(n_cand)), axis=0, keepdims=True)
            hit = pos == first
            sel = jnp.where(hit, 1.0, sel)
            cand = jnp.where(hit, -jnp.inf, cand)
            z = z + jnp.exp(m - m0)
        rank1 = rank_ref[2 * hd]
        jr = jnp.zeros(rank1.shape, F32)
        for r in range(PEER_TOPK):
            if r == 0:
                j_r = jnp.sum(sel[0:PEER_TOPK, :], axis=0, keepdims=True)
            elif r < half:
                j_r = jnp.sum(sel[PEER_TOPK + half * (r - 1):PEER_TOPK + half * r, :], axis=0, keepdims=True)
            else:
                lo = PEER_TOPK + half * (half - 1) + (r - half)
                j_r = sel[lo:lo + 1, :]
            jr = jr + jnp.where(rank1 == float(r), j_r, 0.0)
        jr_ref[hd] = jr
        e1_ref[hd] = jnp.exp(st_ref[2 * hd] - a[0:1, :]) / z
        e2_ref[hd] = jnp.exp(st_ref[2 * hd + 1] - b[0:1, :]).astype(BF16)
        rk2_ref[hd] = rank_ref[2 * hd + 1].astype(BF16)
        return 0

    lax.fori_loop(0, PEER_HEADS, head_body, 0)


def _route(h2, wq, sk):
    n = h2.shape[0]
    tn = _largest_divisor(n, 512, 128)
    tok = lambda i: (0, 0, i)
    tab = lambda dt: jax.ShapeDtypeStruct((PEER_HEADS, PEER_KEYS, n), dt)
    return pl.pallas_call(
        _route_kernel,
        grid=(n // tn,),
        in_specs=[pl.BlockSpec((tn, D_MODEL), lambda i: (i, 0)),
                  pl.BlockSpec((D_MODEL, 2 * PEER_HEADS * PEER_HALF), lambda i: (0, 0)),
                  pl.BlockSpec((2 * PEER_HEADS, PEER_KEYS, PEER_HALF), lambda i: (0, 0, 0))],
        out_specs=[pl.BlockSpec((PEER_HEADS, PEER_KEYS, tn), tok)] * 4,
        out_shape=[tab(F32), tab(F32), tab(BF16), tab(BF16)],
        scratch_shapes=[pltpu.VMEM((2 * PEER_HEADS, PEER_KEYS, tn), F32),
                        pltpu.VMEM((2 * PEER_HEADS, PEER_TOPK, tn), F32),
                        pltpu.VMEM((2 * PEER_HEADS, PEER_KEYS, tn), F32)],
        compiler_params=pltpu.CompilerParams(dimension_semantics=("arbitrary",),
                                             vmem_limit_bytes=VMEM_LIMIT_BYTES),
        name="route",
    )(h2, wq, sk)


def _peer_kernel(h2_ref, u_ref, vt_ref, e1_ref, jr_ref, e2_ref, rk2_ref, hs_ref, o_ref, acc_ref, *, rows_per_step):
    c = pl.program_id(1)

    @pl.when(c == 0)
    def _():
        acc_ref[...] = jnp.zeros_like(acc_ref)

    s_t = _nt_dot(u_ref[...], h2_ref[...])
    act = (0.5 * s_t * (1.0 + lax.erf(s_t * (2.0 ** -0.5)))).astype(BF16)
    shape = (PEER_KEYS, s_t.shape[1])
    ws = []
    for rr in range(rows_per_step):
        gate = jnp.zeros(shape, BF16)
        for hd in range(PEER_HEADS):
            j_r = jnp.broadcast_to(jr_ref[hd, rr:rr + 1, :].astype(BF16), shape)
            e1 = jnp.broadcast_to(e1_ref[hd, rr:rr + 1, :].astype(BF16), shape)
            gate = gate + jnp.where(rk2_ref[hd] < j_r, e2_ref[hd] * e1, jnp.zeros(shape, BF16))
        ws.append(act[rr * PEER_KEYS:(rr + 1) * PEER_KEYS, :] * gate)
    w_t = jnp.concatenate(ws, axis=0)
    acc_ref[...] += jnp.dot(vt_ref[...], w_t, preferred_element_type=F32)

    @pl.when(c == pl.num_programs(1) - 1)
    def _():
        o_ref[...] = hs_ref[...] + acc_ref[...].T


def _peer(h2, u, vt, e1, jr, e2, rk2, hs):
    n = h2.shape[0]
    tn = _largest_divisor(n, 1024, 128)
    rows_per_step = 8
    ec = rows_per_step * PEER_KEYS
    tab = pl.BlockSpec((PEER_HEADS, PEER_KEYS, tn), lambda i, c: (0, 0, i))
    row_tab = pl.BlockSpec((PEER_HEADS, rows_per_step, tn), lambda i, c: (0, c, i))
    return pl.pallas_call(
        functools.partial(_peer_kernel, rows_per_step=rows_per_step),
        grid=(n // tn, PEER_KEYS // rows_per_step),
        in_specs=[pl.BlockSpec((tn, D_MODEL), lambda i, c: (i, 0)),
                  pl.BlockSpec((ec, D_MODEL), lambda i, c: (c, 0)),
                  pl.BlockSpec((D_MODEL, ec), lambda i, c: (0, c)),
                  row_tab, row_tab, tab, tab,
                  pl.BlockSpec((tn, D_MODEL), lambda i, c: (i, 0))],
        out_specs=pl.BlockSpec((tn, D_MODEL), lambda i, c: (i, 0)),
        out_shape=jax.ShapeDtypeStruct((n, D_MODEL), F32),
        scratch_shapes=[pltpu.VMEM((D_MODEL, tn), F32)],
        compiler_params=pltpu.CompilerParams(dimension_semantics=("arbitrary", "arbitrary"),
                                             vmem_limit_bytes=VMEM_LIMIT_BYTES),
        name="peer",
    )(h2, u, vt, e1, jr, e2, rk2, hs)


def _relayout_w_in(w):
    hq = lambda lo: (w[:, lo:lo + 512].reshape(D_MODEL, 2, DIFF_HEADS, HEAD_DIM)
                     .transpose(0, 2, 1, 3).reshape(D_MODEL, 512))
    ff = jnp.pad(w[:, 3072:3080], ((0, 0), (0, 120)))
    return jnp.concatenate([hq(0), hq(512), w[:, 1024:3072], ff, w[:, 3080:5128]], axis=1).astype(BF16)


def kernel(x, meta_tokens, norm_mix_g, w_in, b_gate, b_forget, diff_qnorm_g, diff_knorm_g, diff_lambda,
           diff_subln_g, fox_qnorm_g, fox_knorm_g, w_branch_diff, w_branch_fox, w_out, norm_ffn_g, peer_w_q,
           peer_subkeys, peer_u, peer_v):
    b, s_real, d = x.shape
    assert d == D_MODEL and s_real % 128 == 0 and norm_mix_g.shape[0] == 1
    t_int = s_real + META_ROWS
    layer = 0
    lam_init = 0.8 - 0.6 * math.exp(-0.3 * layer)

    meta_blk = jnp.pad(meta_tokens.astype(F32), ((0, META_ROWS - N_META), (0, 0)))
    hin = jnp.concatenate([x, jnp.broadcast_to(meta_blk[None], (b, META_ROWS, d))], axis=1).reshape(b * t_int, d)

    rows = jnp.arange(t_int, dtype=jnp.int32)
    pos = jnp.where(rows < s_real, rows + N_META, rows - s_real).astype(F32)
    inv_freq = ROPE_THETA ** (-jnp.arange(0, HEAD_DIM, 2, dtype=F32) / HEAD_DIM)
    ang = pos[:, None] * inv_freq[None, :]
    cos = jnp.tile(jnp.cos(ang), (1, 4))
    sin = jnp.tile(jnp.sin(ang), (1, 4))

    tile8 = lambda g: jnp.tile(g.astype(F32), 8)[None, :]
    gidx = jnp.arange(BRANCH_WIDTH) // HEAD_DIM
    gmat = (gidx[:, None] == gidx[None, :]).astype(BF16)
    bf = jnp.pad(b_forget[layer].astype(F32), (0, 120))[None, :]

    dq, dk, dv, fq, fk, fv, lf, ga, gb = _proj(
        hin, norm_mix_g[layer][None, :], _relayout_w_in(w_in[layer]), cos, sin,
        tile8(diff_qnorm_g[layer]), tile8(diff_knorm_g[layer]), tile8(fox_qnorm_g[layer]), tile8(fox_knorm_g[layer]),
        bf, b_gate[layer][None, :], gmat, t_int=t_int, n_valid=s_real + N_META)

    r3 = lambda a: a.reshape(b, t_int, a.shape[-1])
    tri = (jnp.arange(128)[:, None] >= jnp.arange(128)[None, :]).astype(BF16)
    cum_col, cum_row = _cum(r3(lf), tri)

    ya = _diff_attention(r3(dq), r3(dk), r3(dv), diff_lambda[layer].astype(F32),
                         diff_subln_g[layer].astype(F32)[None, :], s_real=s_real, lam_init=lam_init)
    yb = _fox_attention(r3(fq), r3(fk), r3(fv), cum_col, cum_row[:, :, None, :], s_real=s_real)

    hs, h2 = _merge(ya, yb, r3(ga), r3(gb), x, w_branch_diff[layer].astype(BF16), w_branch_fox[layer].astype(BF16),
                    w_out[layer].astype(BF16), norm_ffn_g[layer][None, :])

    n = b * s_real
    h2 = h2.reshape(n, d)
    sk = peer_subkeys[layer].astype(BF16).reshape(2 * PEER_HEADS, PEER_KEYS, PEER_HALF)
    e1, jr, e2, rk2 = _route(h2, peer_w_q[layer].astype(BF16), sk)
    out = _peer(h2, peer_u[layer].astype(BF16), peer_v[layer].astype(BF16).T, e1, jr, e2, rk2, hs.reshape(n, d))
    return out.reshape(b, s_real, d)
```

```python
import functools
import math

import jax
import jax.numpy as jnp
from jax import lax
from jax.experimental import pallas as pl
from jax.experimental.pallas import tpu as pltpu

F32 = jnp.float32
BF16 = jnp.bfloat16

D_MODEL = 1024
N_META = 16
META_ROWS = 128
CHUNK = 64
ROPE_THETA = 10000.0
EPS = 1e-6
NEG = -1e30

DIFF_HEADS = 4
HEAD_DIM = 64
FOX_HEADS = 8
BRANCH_WIDTH = 512

PEER_HEADS = 8
PEER_KEYS = 128
PEER_HALF = 128
PEER_TOPK = 16

_O_DQ, _O_DK, _O_DV, _O_FQ, _O_FK, _O_FV, _O_FF, _O_GA, _O_GB, _O_END = (
    0, 512, 1024, 1536, 2048, 2560, 3072, 3200, 4224, 5248)

VMEM_LIMIT_BYTES = 56 * 1024 * 1024


def _nt_dot(a, b):
    return lax.dot_general(a, b, (((1,), (1,)), ((), ())), preferred_element_type=F32)


def _largest_divisor(n, cap, mult):
    best = None
    for t in range(mult, min(n, cap) + 1, mult):
        if n % t == 0:
            best = t
    assert best is not None, (n, cap, mult)
    return best


def _group_mean_sq(y, gmat):
    v = y * y
    hi = v.astype(BF16)
    lo = (v - hi.astype(F32)).astype(BF16)
    return (jnp.dot(hi, gmat, preferred_element_type=F32)
            + jnp.dot(lo, gmat, preferred_element_type=F32)) * (1.0 / HEAD_DIM)


def _rope(y, cos, sin):
    lane = lax.broadcasted_iota(jnp.int32, y.shape, 1)
    first_half = (lane & (HEAD_DIM - 1)) < (HEAD_DIM // 2)
    w = y.shape[1]
    rot = jnp.where(first_half, -pltpu.roll(y, w - HEAD_DIM // 2, 1), pltpu.roll(y, HEAD_DIM // 2, 1))
    return y * cos + rot * sin


def _proj_kernel(x_ref, g_ref, w_ref, cos_ref, sin_ref, gq_ref, gk_ref, fgq_ref, fgk_ref, bf_ref, bg_ref,
                 gmat_ref, dq_ref, dk_ref, dv_ref, fq_ref, fk_ref, fv_ref, lf_ref, ga_ref, gb_ref,
                 *, tm, blocks_per_batch, n_valid):
    x = x_ref[...]
    h = (x * lax.rsqrt(jnp.mean(x * x, axis=-1, keepdims=True) + EPS) * g_ref[...]).astype(BF16)
    gmat = gmat_ref[...]
    cos = jnp.concatenate([cos_ref[...]] * 4, axis=1)
    sin = jnp.concatenate([sin_ref[...]] * 4, axis=1)

    def sec(lo, hi):
        return jnp.dot(h, w_ref[:, lo:hi], preferred_element_type=F32)

    def headnorm(y, gain):
        return y * lax.rsqrt(_group_mean_sq(y, gmat) + EPS) * gain

    scale = HEAD_DIM ** -0.5
    dq_ref[...] = (_rope(headnorm(sec(_O_DQ, _O_DK), gq_ref[...]), cos, sin) * scale).astype(BF16)
    dk_ref[...] = _rope(headnorm(sec(_O_DK, _O_DV), gk_ref[...]), cos, sin).astype(BF16)
    dv_ref[...] = sec(_O_DV, _O_FQ).astype(BF16)
    fq_ref[...] = (headnorm(sec(_O_FQ, _O_FK), fgq_ref[...]) * scale).astype(BF16)
    fk_ref[...] = headnorm(sec(_O_FK, _O_FV), fgk_ref[...]).astype(BF16)
    fv_ref[...] = sec(_O_FV, _O_FF).astype(BF16)

    z = sec(_O_FF, _O_GA) + bf_ref[...]
    log_f = jnp.minimum(z, 0.0) - jnp.log1p(jnp.exp(-jnp.abs(z)))
    row = (pl.program_id(0) % blocks_per_batch) * tm + lax.broadcasted_iota(jnp.int32, z.shape, 0)
    lf_ref[...] = jnp.where(row < n_valid, log_f, 0.0)

    ga_ref[...] = jax.nn.sigmoid(sec(_O_GA, _O_GB) + bg_ref[:, :D_MODEL]).astype(BF16)
    gb_ref[...] = jax.nn.sigmoid(sec(_O_GB, _O_END) + bg_ref[:, D_MODEL:]).astype(BF16)


def _proj(hin, g, w_all, cos, sin, gq, gk, fgq, fgk, bf, bg, gmat, *, t_int, n_valid):
    n = hin.shape[0]
    tm = _largest_divisor(t_int, 544, 16)
    bpb = t_int // tm
    row = lambda i: (i, 0)
    const = lambda i: (0, 0)
    tab = lambda i: (i % bpb, 0)
    wide = lambda w: pl.BlockSpec((tm, w), row)
    out_shapes = [jax.ShapeDtypeStruct((n, BRANCH_WIDTH), BF16)] * 6 + [
        jax.ShapeDtypeStruct((n, 128), F32),
        jax.ShapeDtypeStruct((n, D_MODEL), BF16), jax.ShapeDtypeStruct((n, D_MODEL), BF16)]
    return pl.pallas_call(
        functools.partial(_proj_kernel, tm=tm, blocks_per_batch=bpb, n_valid=n_valid),
        grid=(n // tm,),
        in_specs=[wide(D_MODEL), pl.BlockSpec((1, D_MODEL), const),
                  pl.BlockSpec((D_MODEL, _O_END), const, pipeline_mode=pl.Buffered(1)),
                  pl.BlockSpec((tm, 128), tab), pl.BlockSpec((tm, 128), tab),
                  pl.BlockSpec((1, BRANCH_WIDTH), const), pl.BlockSpec((1, BRANCH_WIDTH), const),
                  pl.BlockSpec((1, BRANCH_WIDTH), const), pl.BlockSpec((1, BRANCH_WIDTH), const),
                  pl.BlockSpec((1, 128), const), pl.BlockSpec((1, 2 * D_MODEL), const),
                  pl.BlockSpec((BRANCH_WIDTH, BRANCH_WIDTH), const)],
        out_specs=[wide(BRANCH_WIDTH)] * 6 + [wide(128), wide(D_MODEL), wide(D_MODEL)],
        out_shape=out_shapes,
        compiler_params=pltpu.CompilerParams(dimension_semantics=("arbitrary",),
                                             vmem_limit_bytes=VMEM_LIMIT_BYTES),
        name="proj",
    )(hin, g, w_all, cos, sin, gq, gk, fgq, fgk, bf, bg, gmat)


def _cum_kernel(lf_ref, tri_ref, col_ref, row_ref, *, n_blocks):
    tri = tri_ref[...]
    carry = jnp.zeros((1, 128), F32)
    for blk in [n_blocks - 1] + list(range(n_blocks - 1)):
        sl = slice(blk * 128, (blk + 1) * 128)
        v = lf_ref[0, sl, :]
        h1 = v.astype(BF16)
        r1 = v - h1.astype(F32)
        h2 = r1.astype(BF16)
        h3 = (r1 - h2.astype(F32)).astype(BF16)
        c = (jnp.dot(tri, h1, preferred_element_type=F32) + jnp.dot(tri, h2, preferred_element_type=F32)
             + jnp.dot(tri, h3, preferred_element_type=F32)) + carry
        col_ref[0, sl, :] = c
        row_ref[0, :, sl] = c.T[0:FOX_HEADS, :]
        carry = c[127:128, :]


def _cum(lf3, tri):
    b, t_int, _ = lf3.shape
    return pl.pallas_call(
        functools.partial(_cum_kernel, n_blocks=t_int // 128),
        grid=(b,),
        in_specs=[pl.BlockSpec((1, t_int, 128), lambda i: (i, 0, 0)), pl.BlockSpec((128, 128), lambda i: (0, 0))],
        out_specs=[pl.BlockSpec((1, t_int, 128), lambda i: (i, 0, 0)),
                   pl.BlockSpec((1, FOX_HEADS, t_int), lambda i: (i, 0, 0))],
        out_shape=[jax.ShapeDtypeStruct((b, t_int, 128), F32), jax.ShapeDtypeStruct((b, FOX_HEADS, t_int), F32)],
        compiler_params=pltpu.CompilerParams(dimension_semantics=("arbitrary",)),
        name="cum",
    )(lf3, tri)


def _online_softmax_step(qg, kblk, vblk, mask, bias, m, l, acc):
    s = _nt_dot(qg, kblk)
    if bias is not None:
        s = s + bias
    if mask is not None:
        s = jnp.where(mask, s, NEG)
    m_new = jnp.maximum(m, jnp.max(s, axis=1, keepdims=True))
    p = jnp.exp(s - m_new)
    alpha = jnp.exp(m - m_new)
    l = alpha * l + jnp.sum(p, axis=1, keepdims=True)
    acc = alpha * acc + jnp.dot(p.astype(BF16), vblk, preferred_element_type=F32)
    return m_new, l, acc


def _first_softmax_step(qg, kblk, vblk, mask, bias):
    s = _nt_dot(qg, kblk)
    if bias is not None:
        s = s + bias
    s = jnp.where(mask, s, NEG)
    m = jnp.max(s, axis=1, keepdims=True)
    p = jnp.exp(s - m)
    return m, jnp.sum(p, axis=1, keepdims=True), jnp.dot(p.astype(BF16), vblk, preferred_element_type=F32)


def _split_halves(q_ref, n_groups):
    qs = []
    for g in range(n_groups):
        q = q_ref[0, :, g * 128:(g + 1) * 128]
        lo = lax.broadcasted_iota(jnp.int32, q.shape, 1) < HEAD_DIM
        qs += [jnp.where(lo, q, jnp.zeros_like(q)), jnp.where(lo, jnp.zeros_like(q), q)]
    return qs


def _attention_sweep(q_ref, k_ref, v_ref, i, *, tq, s_real, n_groups, diag_limit_fn, bias_fn):
    qs = _split_halves(q_ref, n_groups)
    k0 = pl.multiple_of(i * tq, tq)
    r = lax.broadcasted_iota(jnp.int32, (tq, tq + META_ROWS), 0)
    c = lax.broadcasted_iota(jnp.int32, (tq, tq + META_ROWS), 1)
    mask = c < jnp.where(c < tq, diag_limit_fn(r), tq + N_META)
    carry = []
    for g in range(n_groups):
        cols = slice(g * 128, (g + 1) * 128)
        kk = jnp.concatenate([k_ref[0, pl.ds(k0, tq), cols], k_ref[0, s_real:s_real + META_ROWS, cols]], axis=0)
        vv = jnp.concatenate([v_ref[0, pl.ds(k0, tq), cols], v_ref[0, s_real:s_real + META_ROWS, cols]], axis=0)
        for e in range(2):
            bias = None if bias_fn is None else jnp.concatenate(
                [bias_fn(2 * g + e, k0, tq), bias_fn(2 * g + e, s_real, META_ROWS)], axis=1)
            carry.append(_first_softmax_step(qs[2 * g + e], kk, vv, mask, bias))

    def body(kc, carry):
        kb = pl.multiple_of(kc * tq, tq)
        out = []
        for g in range(n_groups):
            cols = slice(g * 128, (g + 1) * 128)
            kblk = k_ref[0, pl.ds(kb, tq), cols]
            vblk = v_ref[0, pl.ds(kb, tq), cols]
            for e in range(2):
                bias = None if bias_fn is None else bias_fn(2 * g + e, kb, tq)
                out.append(_online_softmax_step(qs[2 * g + e], kblk, vblk, None, bias, *carry[2 * g + e]))
        return tuple(out)

    return lax.fori_loop(0, i, body, tuple(carry))


def _diff_attn_kernel(q_ref, k_ref, v_ref, lam_ref, sg_ref, o_ref, *, tq, s_real, lam_init):
    carry = _attention_sweep(q_ref, k_ref, v_ref, pl.program_id(1), tq=tq, s_real=s_real, n_groups=DIFF_HEADS,
                             diag_limit_fn=lambda r: (r // CHUNK + 1) * CHUNK, bias_fn=None)
    lam = lam_ref[...]
    lam = (jnp.exp(jnp.sum(lam[0:1] * lam[1:2], axis=1, keepdims=True))
           - jnp.exp(jnp.sum(lam[2:3] * lam[3:4], axis=1, keepdims=True)) + lam_init)
    for h in range(DIFF_HEADS):
        (_, l0, a0), (_, l1, a1) = carry[2 * h], carry[2 * h + 1]
        od = a0 / l0 - lam * (a1 / l1)
        y = od * lax.rsqrt(jnp.mean(od * od, axis=-1, keepdims=True) + EPS) * sg_ref[...]
        o_ref[0, :, h * 128:(h + 1) * 128] = (y * (1.0 - lam_init)).astype(BF16)


def _attention_call(kernel_fn, name, q, k, v, extra, extra_specs, *, s_real):
    b, t_int, w = q.shape
    tq = min(256, s_real)
    return pl.pallas_call(
        functools.partial(kernel_fn, tq=tq, s_real=s_real),
        grid=(b, s_real // tq),
        in_specs=[pl.BlockSpec((1, tq, w), lambda bi, i: (bi, i, 0)),
                  pl.BlockSpec((1, t_int, w), lambda bi, i: (bi, 0, 0)),
                  pl.BlockSpec((1, t_int, w), lambda bi, i: (bi, 0, 0))] + extra_specs(tq, t_int),
        out_specs=pl.BlockSpec((1, tq, w), lambda bi, i: (bi, i, 0)),
        out_shape=jax.ShapeDtypeStruct((b, s_real, w), BF16),
        compiler_params=pltpu.CompilerParams(dimension_semantics=("arbitrary",) * 2,
                                             vmem_limit_bytes=VMEM_LIMIT_BYTES),
        name=name,
    )(q, k, v, *extra)


def _diff_attention(q, k, v, lam, sg, *, s_real, lam_init):
    specs = lambda tq, t_int: [pl.BlockSpec((4, HEAD_DIM), lambda bi, i: (0, 0)),
                               pl.BlockSpec((1, 128), lambda bi, i: (0, 0))]
    return _attention_call(functools.partial(_diff_attn_kernel, lam_init=lam_init), "diff_attn", q, k, v,
                           (lam, sg), specs, s_real=s_real)


def _fox_attn_kernel(q_ref, k_ref, v_ref, col_ref, row_ref, o_ref, *, tq, s_real):
    cum = col_ref[0]

    def bias_fn(head, k0, width):
        return cum[:, head:head + 1] - row_ref[0, head, :, pl.ds(k0, width)]

    carry = _attention_sweep(q_ref, k_ref, v_ref, pl.program_id(1), tq=tq, s_real=s_real,
                             n_groups=FOX_HEADS // 2, diag_limit_fn=lambda r: r + 1, bias_fn=bias_fn)
    lo = lax.broadcasted_iota(jnp.int32, (tq, 128), 1) < HEAD_DIM
    for g in range(FOX_HEADS // 2):
        (_, l0, a0), (_, l1, a1) = carry[2 * g], carry[2 * g + 1]
        o_ref[0, :, g * 128:(g + 1) * 128] = jnp.where(lo, a0 / l0, a1 / l1).astype(BF16)


def _fox_attention(q, k, v, cum_col, cum_row, *, s_real):
    specs = lambda tq, t_int: [pl.BlockSpec((1, tq, 128), lambda bi, i: (bi, i, 0)),
                               pl.BlockSpec((1, FOX_HEADS, 1, t_int), lambda bi, i: (bi, 0, 0, 0))]
    return _attention_call(_fox_attn_kernel, "fox_attn", q, k, v, (cum_col, cum_row), specs, s_real=s_real)


def _merge_kernel(ya_ref, yb_ref, ga_ref, gb_ref, x_ref, wd_ref, wf_ref, wo_ref, g_ref, hs_ref, h2_ref):
    merged = (ga_ref[0].astype(F32) * jnp.dot(ya_ref[0], wd_ref[...], preferred_element_type=F32)
              + gb_ref[0].astype(F32) * jnp.dot(yb_ref[0], wf_ref[...], preferred_element_type=F32))
    hs = x_ref[0] + jnp.dot(merged.astype(BF16), wo_ref[...], preferred_element_type=F32)
    hs_ref[0] = hs
    h2_ref[0] = (hs * lax.rsqrt(jnp.mean(hs * hs, axis=-1, keepdims=True) + EPS) * g_ref[...]).astype(BF16)


def _merge(ya, yb, ga, gb, x, wd, wf, wo, g):
    b, s_real, _ = x.shape
    tm = min(512, s_real)
    blk = lambda w: pl.BlockSpec((1, tm, w), lambda bi, i: (bi, i, 0))
    const = lambda shape: pl.BlockSpec(shape, lambda bi, i: (0, 0))
    return pl.pallas_call(
        _merge_kernel,
        grid=(b, s_real // tm),
        in_specs=[blk(BRANCH_WIDTH), blk(BRANCH_WIDTH), blk(D_MODEL), blk(D_MODEL), blk(D_MODEL),
                  const((BRANCH_WIDTH, D_MODEL)), const((BRANCH_WIDTH, D_MODEL)), const((D_MODEL, D_MODEL)),
                  const((1, D_MODEL))],
        out_specs=[blk(D_MODEL), blk(D_MODEL)],
        out_shape=[jax.ShapeDtypeStruct((b, s_real, D_MODEL), F32), jax.ShapeDtypeStruct((b, s_real, D_MODEL), BF16)],
        compiler_params=pltpu.CompilerParams(dimension_semantics=("arbitrary",) * 2,
                                             vmem_limit_bytes=VMEM_LIMIT_BYTES),
        name="merge",
    )(ya, yb, ga, gb, x, wd, wf, wo, g)


def _route_kernel(h2_ref, wq_ref, sk_ref, e1_ref, jr_ref, e2_ref, rk2_ref, st_ref, val_ref, rank_ref):
    n_lists = 2 * PEER_HEADS
    tn = h2_ref.shape[0]
    q = jnp.dot(h2_ref[...], wq_ref[...], preferred_element_type=F32).astype(BF16)
    for li in range(n_lists):
        st_ref[li] = _nt_dot(sk_ref[li], q[:, li * PEER_HALF:(li + 1) * PEER_HALF])

    def extract(s, index, index_end, on_hit, tie_exact):
        for it in range(PEER_TOPK):
            m = jnp.max(s, axis=0, keepdims=True)
            hit = s == m
            if tie_exact:
                hit = index == jnp.min(jnp.where(hit, index, index_end), axis=0, keepdims=True)
            on_hit(it, m, hit)
            s = jnp.where(hit, -jnp.inf, s)

    def any_tie(removed):
        return jnp.max(jnp.sum(removed, axis=0, keepdims=True)) > float(PEER_TOPK)

    rows = lax.broadcasted_iota(jnp.int32, (PEER_KEYS, tn), 0).astype(F32)

    def list_body(li, _):
        def run(tie_exact):
            state = {"rank": jnp.full((PEER_KEYS, tn), float(PEER_TOPK), F32), "vals": []}

            def on_hit(it, m, hit):
                state["rank"] = jnp.where(hit, float(it), state["rank"])
                state["vals"].append(m)

            extract(st_ref[li], rows, float(PEER_KEYS), on_hit, tie_exact)
            val_ref[li] = jnp.concatenate(state["vals"], axis=0)
            rank_ref[li] = state["rank"]
            return state["rank"]

        rank = run(False)

        @pl.when(any_tie(jnp.where(rank < float(PEER_TOPK), 1.0, 0.0)))
        def _():
            run(True)

        return 0

    lax.fori_loop(0, n_lists, list_body, 0)

    n_cand = PEER_TOPK * PEER_TOPK
    half = PEER_TOPK // 2
    iota8 = lax.broadcasted_iota(jnp.int32, (half, tn), 0)
    pos = jnp.concatenate(
        [lax.broadcasted_iota(jnp.int32, (PEER_TOPK, tn), 0)]
        + [iota8 + PEER_TOPK * r for r in range(1, half)]
        + [(iota8 + half) * PEER_TOPK], axis=0).astype(F32)

    def head_body(hd, _):
        a = val_ref[2 * hd]
        b = val_ref[2 * hd + 1]
        cand = jnp.concatenate(
            [a[0:1, :] + b] + [a[r:r + 1, :] + b[0:half, :] for r in range(1, half)]
            + [a[half:PEER_TOPK, :] + b[0:1, :]], axis=0)
        m0 = a[0:1, :] + b[0:1, :]

        def run(tie_exact):
            state = {"sel": jnp.zeros(cand.shape, F32), "z": jnp.zeros((1, tn), F32)}

            def on_hit(it, m, hit):
                state["sel"] = jnp.where(hit, 1.0, state["sel"])
                state["z"] = state["z"] + jnp.exp(m - m0)

            extract(cand, pos, float(n_cand), on_hit, tie_exact)
            sel = state["sel"]
            rank1 = rank_ref[2 * hd]
            jr = jnp.zeros(rank1.shape, F32)
            for r in range(PEER_TOPK):
                if r == 0:
                    j_r = jnp.sum(sel[0:PEER_TOPK, :], axis=0, keepdims=True)
                elif r < half:
                    j_r = jnp.sum(sel[PEER_TOPK + half * (r - 1):PEER_TOPK + half * r, :], axis=0, keepdims=True)
                else:
                    lo = PEER_TOPK + half * (half - 1) + (r - half)
                    j_r = sel[lo:lo + 1, :]
                jr = jnp.where(rank1 == float(r), j_r, jr)
            jr_ref[hd] = jr
            e1_ref[hd] = jnp.exp(st_ref[2 * hd] - a[0:1, :]) / state["z"]
            return sel

        sel = run(False)

        @pl.when(any_tie(sel))
        def _():
            run(True)

        e2_ref[hd] = jnp.exp(st_ref[2 * hd + 1] - b[0:1, :]).astype(BF16)
        rk2_ref[hd] = rank_ref[2 * hd + 1].astype(BF16)
        return 0

    lax.fori_loop(0, PEER_HEADS, head_body, 0)


def _route(h2, wq, sk):
    n = h2.shape[0]
    tn = _largest_divisor(n, 512, 128)
    tok = lambda i: (0, 0, i)
    tab = lambda dt: jax.ShapeDtypeStruct((PEER_HEADS, PEER_KEYS, n), dt)
    return pl.pallas_call(
        _route_kernel,
        grid=(n // tn,),
        in_specs=[pl.BlockSpec((tn, D_MODEL), lambda i: (i, 0)),
                  pl.BlockSpec((D_MODEL, 2 * PEER_HEADS * PEER_HALF), lambda i: (0, 0)),
                  pl.BlockSpec((2 * PEER_HEADS, PEER_KEYS, PEER_HALF), lambda i: (0, 0, 0))],
        out_specs=[pl.BlockSpec((PEER_HEADS, PEER_KEYS, tn), tok)] * 4,
        out_shape=[tab(F32), tab(F32), tab(BF16), tab(BF16)],
        scratch_shapes=[pltpu.VMEM((2 * PEER_HEADS, PEER_KEYS, tn), F32),
                        pltpu.VMEM((2 * PEER_HEADS, PEER_TOPK, tn), F32),
                        pltpu.VMEM((2 * PEER_HEADS, PEER_KEYS, tn), F32)],
        compiler_params=pltpu.CompilerParams(dimension_semantics=("arbitrary",),
                                             vmem_limit_bytes=VMEM_LIMIT_BYTES),
        name="route",
    )(h2, wq, sk)


def _peer_kernel(h2_ref, u_ref, vta_ref, vtb_ref, e1_ref, jr_ref, e2_ref, rk2_ref, hs_ref, o_ref, acc_ref, w_ref,
                 *, rows_per_step):
    c = pl.program_id(1)
    n_chunks = pl.num_programs(1) - 1
    tn = h2_ref.shape[0]
    shape = (PEER_KEYS, tn)
    n_sub = rows_per_step // 2

    @pl.when(c == 0)
    def _():
        acc_ref[...] = jnp.zeros_like(acc_ref)
        w_ref[...] = jnp.zeros_like(w_ref)

    h2 = h2_ref[...]
    live = jnp.where(c < n_chunks, 1.0, 0.0)
    rows16 = lambda row: jnp.concatenate([jnp.broadcast_to(row, (16, tn)).astype(BF16)] * (PEER_KEYS // 16), axis=0)

    def sub_chain(p):
        lo = 2 * p * PEER_KEYS
        s_t = _nt_dot(u_ref[lo:lo + 2 * PEER_KEYS, :], h2)
        act = (0.5 * s_t * (1.0 + lax.erf(s_t * (2.0 ** -0.5)))).astype(BF16)
        out = []
        for rr in (2 * p, 2 * p + 1):
            gate = jnp.zeros(shape, BF16)
            for hd in range(PEER_HEADS):
                gate = gate + jnp.where(rk2_ref[hd] < rows16(jr_ref[hd, rr:rr + 1, :]),
                                        e2_ref[hd] * rows16(e1_ref[hd, rr:rr + 1, :] * live), jnp.zeros(shape, BF16))
            out.append(act[(rr - 2 * p) * PEER_KEYS:(rr - 2 * p + 1) * PEER_KEYS, :] * gate)
        return jnp.concatenate(out, axis=0)

    first = [sub_chain(p) for p in range(n_sub // 2)]
    acc_ref[...] += jnp.dot(vtb_ref[...], w_ref[...], preferred_element_type=F32)
    second = [sub_chain(p) for p in range(n_sub // 2, n_sub)]
    acc_ref[...] += jnp.dot(vta_ref[...], jnp.concatenate(first, axis=0), preferred_element_type=F32)
    w_ref[...] = jnp.concatenate(second, axis=0)

    @pl.when(c == n_chunks)
    def _():
        o_ref[...] = hs_ref[...] + acc_ref[...].T


def _peer(h2, u, vt, e1, jr, e2, rk2, hs):
    n = h2.shape[0]
    tn = _largest_divisor(n, 1024, 128)
    rows_per_step = 8
    ec = rows_per_step * PEER_KEYS
    n_chunks = PEER_KEYS // rows_per_step
    cur = lambda c: jnp.minimum(c, n_chunks - 1)
    half = ec // 2
    tab = pl.BlockSpec((PEER_HEADS, PEER_KEYS, tn), lambda i, c: (0, 0, i))
    row_tab = pl.BlockSpec((PEER_HEADS, rows_per_step, tn), lambda i, c: (0, cur(c), i))
    return pl.pallas_call(
        functools.partial(_peer_kernel, rows_per_step=rows_per_step),
        grid=(n // tn, n_chunks + 1),
        in_specs=[pl.BlockSpec((tn, D_MODEL), lambda i, c: (i, 0)),
                  pl.BlockSpec((ec, D_MODEL), lambda i, c: (cur(c), 0)),
                  pl.BlockSpec((D_MODEL, half), lambda i, c: (0, 2 * cur(c))),
                  pl.BlockSpec((D_MODEL, half), lambda i, c: (0, jnp.maximum(2 * c - 1, 0))),
                  row_tab, row_tab, tab, tab,
                  pl.BlockSpec((tn, D_MODEL), lambda i, c: (i, 0))],
        out_specs=pl.BlockSpec((tn, D_MODEL), lambda i, c: (i, 0)),
        out_shape=jax.ShapeDtypeStruct((n, D_MODEL), F32),
        scratch_shapes=[pltpu.VMEM((D_MODEL, tn), F32), pltpu.VMEM((half, tn), BF16)],
        compiler_params=pltpu.CompilerParams(dimension_semantics=("arbitrary", "arbitrary"),
                                             vmem_limit_bytes=VMEM_LIMIT_BYTES),
        name="peer",
    )(h2, u, vt, vt, e1, jr, e2, rk2, hs)


def _relayout_w_in(w):
    hq = lambda lo: (w[:, lo:lo + 512].reshape(D_MODEL, 2, DIFF_HEADS, HEAD_DIM)
                     .transpose(0, 2, 1, 3).reshape(D_MODEL, 512))
    ff = jnp.pad(w[:, 3072:3080], ((0, 0), (0, 120)))
    return jnp.concatenate([hq(0), hq(512), w[:, 1024:3072], ff, w[:, 3080:5128]], axis=1).astype(BF16)


def kernel(x, meta_tokens, norm_mix_g, w_in, b_gate, b_forget, diff_qnorm_g, diff_knorm_g, diff_lambda,
           diff_subln_g, fox_qnorm_g, fox_knorm_g, w_branch_diff, w_branch_fox, w_out, norm_ffn_g, peer_w_q,
           peer_subkeys, peer_u, peer_v):
    b, s_real, d = x.shape
    assert d == D_MODEL and s_real % 128 == 0 and norm_mix_g.shape[0] == 1
    t_int = s_real + META_ROWS
    layer = 0
    lam_init = 0.8 - 0.6 * math.exp(-0.3 * layer)

    meta_blk = jnp.pad(meta_tokens.astype(F32), ((0, META_ROWS - N_META), (0, 0)))
    hin = jnp.concatenate([x, jnp.broadcast_to(meta_blk[None], (b, META_ROWS, d))], axis=1).reshape(b * t_int, d)

    rows = jnp.arange(t_int, dtype=jnp.int32)
    pos = jnp.where(rows < s_real, rows + N_META, rows - s_real).astype(F32)
    inv_freq = ROPE_THETA ** (-jnp.arange(0, HEAD_DIM, 2, dtype=F32) / HEAD_DIM)
    ang = pos[:, None] * inv_freq[None, :]
    cos = jnp.tile(jnp.cos(ang), (1, 4))
    sin = jnp.tile(jnp.sin(ang), (1, 4))

    tile8 = lambda g: jnp.tile(g.astype(F32), 8)[None, :]
    gidx = jnp.arange(BRANCH_WIDTH) // HEAD_DIM
    gmat = (gidx[:, None] == gidx[None, :]).astype(BF16)
    bf = jnp.pad(b_forget[layer].astype(F32), (0, 120))[None, :]

    dq, dk, dv, fq, fk, fv, lf, ga, gb = _proj(
        hin, norm_mix_g[layer][None, :], _relayout_w_in(w_in[layer]), cos, sin,
        tile8(diff_qnorm_g[layer]), tile8(diff_knorm_g[layer]), tile8(fox_qnorm_g[layer]), tile8(fox_knorm_g[layer]),
        bf, b_gate[layer][None, :], gmat, t_int=t_int, n_valid=s_real + N_META)

    r3 = lambda a: a.reshape(b, t_int, a.shape[-1])
    tri = (jnp.arange(128)[:, None] >= jnp.arange(128)[None, :]).astype(BF16)
    cum_col, cum_row = _cum(r3(lf), tri)

    ya = _diff_attention(r3(dq), r3(dk), r3(dv), diff_lambda[layer].astype(F32),
                         diff_subln_g[layer].astype(F32)[None, :], s_real=s_real, lam_init=lam_init)
    yb = _fox_attention(r3(fq), r3(fk), r3(fv), cum_col, cum_row[:, :, None, :], s_real=s_real)

    hs, h2 = _merge(ya, yb, r3(ga), r3(gb), x, w_branch_diff[layer].astype(BF16), w_branch_fox[layer].astype(BF16),
                    w_out[layer].astype(BF16), norm_ffn_g[layer][None, :])

    n = b * s_real
    h2 = h2.reshape(n, d)
    sk = peer_subkeys[layer].astype(BF16).reshape(2 * PEER_HEADS, PEER_KEYS, PEER_HALF)
    e1, jr, e2, rk2 = _route(h2, peer_w_q[layer].astype(BF16), sk)
    out = _peer(h2, peer_u[layer].astype(BF16), peer_v[layer].astype(BF16).T, e1, jr, e2, rk2, hs.reshape(n, d))
    return out.reshape(b, s_real, d)
```

```python
import functools
import math

import jax
import jax.numpy as jnp
from jax import lax
from jax.experimental import pallas as pl
from jax.experimental.pallas import tpu as pltpu

F32 = jnp.float32
BF16 = jnp.bfloat16

D_MODEL = 1024
N_META = 16
META_ROWS = 128
CHUNK = 64
ROPE_THETA = 10000.0
EPS = 1e-6
NEG = -1e30
LOG2E = math.log2(math.e)
MAX_FIXED_SHIFT = 60.0

DIFF_HEADS = 4
HEAD_DIM = 64
FOX_HEADS = 8
BRANCH_WIDTH = 512

PEER_HEADS = 8
PEER_KEYS = 128
PEER_HALF = 128
PEER_TOPK = 16

_O_DQ, _O_DK, _O_DV, _O_FQ, _O_FK, _O_FV, _O_FF, _O_GA, _O_GB, _O_END = (
    0, 512, 1024, 1536, 2048, 2560, 3072, 3200, 4224, 5248)

VMEM_LIMIT_BYTES = 56 * 1024 * 1024


def _nt_dot(a, b):
    return lax.dot_general(a, b, (((1,), (1,)), ((), ())), preferred_element_type=F32)


def _largest_divisor(n, cap, mult):
    best = None
    for t in range(mult, min(n, cap) + 1, mult):
        if n % t == 0:
            best = t
    assert best is not None, (n, cap, mult)
    return best


def _group_mean_sq(y, gmat):
    v = y * y
    hi = v.astype(BF16)
    lo = (v - hi.astype(F32)).astype(BF16)
    return (jnp.dot(hi, gmat, preferred_element_type=F32)
            + jnp.dot(lo, gmat, preferred_element_type=F32)) * (1.0 / HEAD_DIM)


def _rope(y, cos, sin):
    lane = lax.broadcasted_iota(jnp.int32, y.shape, 1)
    first_half = (lane & (HEAD_DIM - 1)) < (HEAD_DIM // 2)
    w = y.shape[1]
    rot = jnp.where(first_half, -pltpu.roll(y, w - HEAD_DIM // 2, 1), pltpu.roll(y, HEAD_DIM // 2, 1))
    return y * cos + rot * sin


def _proj_kernel(x_ref, g_ref, w_ref, cos_ref, sin_ref, gq_ref, gk_ref, fgq_ref, fgk_ref, bf_ref, bg_ref,
                 gmat_ref, dq_ref, dk_ref, dv_ref, fq_ref, fk_ref, fv_ref, lf_ref, ga_ref, gb_ref,
                 *, tm, blocks_per_batch, n_valid):
    x = x_ref[...]
    h = (x * lax.rsqrt(jnp.mean(x * x, axis=-1, keepdims=True) + EPS) * g_ref[...]).astype(BF16)
    gmat = gmat_ref[...]
    cos = jnp.concatenate([cos_ref[...]] * 4, axis=1)
    sin = jnp.concatenate([sin_ref[...]] * 4, axis=1)

    def sec(lo, hi):
        return jnp.dot(h, w_ref[:, lo:hi], preferred_element_type=F32)

    def headnorm(y, gain):
        return y * lax.rsqrt(_group_mean_sq(y, gmat) + EPS) * gain

    scale = HEAD_DIM ** -0.5 * LOG2E
    dq_ref[...] = (_rope(headnorm(sec(_O_DQ, _O_DK), gq_ref[...]), cos, sin) * scale).astype(BF16)
    dk_ref[...] = _rope(headnorm(sec(_O_DK, _O_DV), gk_ref[...]), cos, sin).astype(BF16)
    dv_ref[...] = sec(_O_DV, _O_FQ).astype(BF16)
    fq_ref[...] = (headnorm(sec(_O_FQ, _O_FK), fgq_ref[...]) * scale).astype(BF16)
    fk_ref[...] = headnorm(sec(_O_FK, _O_FV), fgk_ref[...]).astype(BF16)
    fv_ref[...] = sec(_O_FV, _O_FF).astype(BF16)

    z = sec(_O_FF, _O_GA) + bf_ref[...]
    log_f = jnp.minimum(z, 0.0) - jnp.log1p(jnp.exp(-jnp.abs(z)))
    row = (pl.program_id(0) % blocks_per_batch) * tm + lax.broadcasted_iota(jnp.int32, z.shape, 0)
    lf_ref[...] = jnp.where(row < n_valid, log_f, 0.0)

    ga_ref[...] = jax.nn.sigmoid(sec(_O_GA, _O_GB) + bg_ref[:, :D_MODEL]).astype(BF16)
    gb_ref[...] = jax.nn.sigmoid(sec(_O_GB, _O_END) + bg_ref[:, D_MODEL:]).astype(BF16)


def _proj(hin, g, w_all, cos, sin, gq, gk, fgq, fgk, bf, bg, gmat, *, t_int, n_valid):
    n = hin.shape[0]
    tm = _largest_divisor(t_int, 544, 16)
    bpb = t_int // tm
    row = lambda i: (i, 0)
    const = lambda i: (0, 0)
    tab = lambda i: (i % bpb, 0)
    wide = lambda w: pl.BlockSpec((tm, w), row)
    out_shapes = [jax.ShapeDtypeStruct((n, BRANCH_WIDTH), BF16)] * 6 + [
        jax.ShapeDtypeStruct((n, 128), F32),
        jax.ShapeDtypeStruct((n, D_MODEL), BF16), jax.ShapeDtypeStruct((n, D_MODEL), BF16)]
    return pl.pallas_call(
        functools.partial(_proj_kernel, tm=tm, blocks_per_batch=bpb, n_valid=n_valid),
        grid=(n // tm,),
        in_specs=[wide(D_MODEL), pl.BlockSpec((1, D_MODEL), const),
                  pl.BlockSpec((D_MODEL, _O_END), const, pipeline_mode=pl.Buffered(1)),
                  pl.BlockSpec((tm, 128), tab), pl.BlockSpec((tm, 128), tab),
                  pl.BlockSpec((1, BRANCH_WIDTH), const), pl.BlockSpec((1, BRANCH_WIDTH), const),
                  pl.BlockSpec((1, BRANCH_WIDTH), const), pl.BlockSpec((1, BRANCH_WIDTH), const),
                  pl.BlockSpec((1, 128), const), pl.BlockSpec((1, 2 * D_MODEL), const),
                  pl.BlockSpec((BRANCH_WIDTH, BRANCH_WIDTH), const)],
        out_specs=[wide(BRANCH_WIDTH)] * 6 + [wide(128), wide(D_MODEL), wide(D_MODEL)],
        out_shape=out_shapes,
        compiler_params=pltpu.CompilerParams(dimension_semantics=("arbitrary",),
                                             vmem_limit_bytes=VMEM_LIMIT_BYTES),
        name="proj",
    )(hin, g, w_all, cos, sin, gq, gk, fgq, fgk, bf, bg, gmat)


def _cum_kernel(lf_ref, tri_ref, col_ref, row_ref, *, n_blocks):
    tri = tri_ref[...]
    carry = jnp.zeros((1, 128), F32)
    for blk in [n_blocks - 1] + list(range(n_blocks - 1)):
        sl = slice(blk * 128, (blk + 1) * 128)
        v = lf_ref[0, sl, :] * LOG2E
        h1 = v.astype(BF16)
        r1 = v - h1.astype(F32)
        h2 = r1.astype(BF16)
        h3 = (r1 - h2.astype(F32)).astype(BF16)
        c = (jnp.dot(tri, h1, preferred_element_type=F32) + jnp.dot(tri, h2, preferred_element_type=F32)
             + jnp.dot(tri, h3, preferred_element_type=F32)) + carry
        col_ref[0, sl, :] = c
        row_ref[0, :, sl] = c.T[0:FOX_HEADS, :]
        carry = c[127:128, :]


def _cum(lf3, tri):
    b, t_int, _ = lf3.shape
    return pl.pallas_call(
        functools.partial(_cum_kernel, n_blocks=t_int // 128),
        grid=(b,),
        in_specs=[pl.BlockSpec((1, t_int, 128), lambda i: (i, 0, 0)), pl.BlockSpec((128, 128), lambda i: (0, 0))],
        out_specs=[pl.BlockSpec((1, t_int, 128), lambda i: (i, 0, 0)),
                   pl.BlockSpec((1, FOX_HEADS, t_int), lambda i: (i, 0, 0))],
        out_shape=[jax.ShapeDtypeStruct((b, t_int, 128), F32), jax.ShapeDtypeStruct((b, FOX_HEADS, t_int), F32)],
        compiler_params=pltpu.CompilerParams(dimension_semantics=("arbitrary",)),
        name="cum",
    )(lf3, tri)


def _softmax_step(qg, kblk, vblk, mask, bias, state, shift):
    s = _nt_dot(qg, kblk)
    if bias is not None:
        s = s + bias
    if mask is not None:
        s = jnp.where(mask, s, NEG)
    if shift is not None:
        p = jnp.exp2(s - shift)
        part = p[:, 0:128]
        for j in range(1, p.shape[1] // 128):
            part = part + p[:, j * 128:(j + 1) * 128]
        pv = jnp.dot(p.astype(BF16), vblk, preferred_element_type=F32)
        return (part, pv) if state is None else (state[0] + part, state[1] + pv)
    blk_max = jnp.max(s, axis=1, keepdims=True)
    if state is None:
        p = jnp.exp2(s - blk_max)
        return (blk_max, jnp.sum(p, axis=1, keepdims=True),
                jnp.dot(p.astype(BF16), vblk, preferred_element_type=F32))
    m_old, l, acc = state
    m = jnp.maximum(m_old, blk_max)
    p = jnp.exp2(s - m)
    alpha = jnp.exp2(m_old - m)
    return (m, alpha * l + jnp.sum(p, axis=1, keepdims=True),
            alpha * acc + jnp.dot(p.astype(BF16), vblk, preferred_element_type=F32))


def _split_halves(q_ref, n_groups):
    qs = []
    for g in range(n_groups):
        q = q_ref[0, :, g * 128:(g + 1) * 128]
        lo = lax.broadcasted_iota(jnp.int32, q.shape, 1) < HEAD_DIM
        qs += [jnp.where(lo, q, jnp.zeros_like(q)), jnp.where(lo, jnp.zeros_like(q), q)]
    return qs


def _attention_sweep(q_ref, k_ref, v_ref, i, shift, *, tq, s_real, n_groups, diag_limit_fn, bias_fn):
    qs = _split_halves(q_ref, n_groups)
    k0 = pl.multiple_of(i * tq, tq)
    r = lax.broadcasted_iota(jnp.int32, (tq, tq + META_ROWS), 0)
    c = lax.broadcasted_iota(jnp.int32, (tq, tq + META_ROWS), 1)
    mask = c < jnp.where(c < tq, diag_limit_fn(r), tq + N_META)
    carry = []
    for g in range(n_groups):
        cols = slice(g * 128, (g + 1) * 128)
        kk = jnp.concatenate([k_ref[0, pl.ds(k0, tq), cols], k_ref[0, s_real:s_real + META_ROWS, cols]], axis=0)
        vv = jnp.concatenate([v_ref[0, pl.ds(k0, tq), cols], v_ref[0, s_real:s_real + META_ROWS, cols]], axis=0)
        for e in range(2):
            bias = None if bias_fn is None else jnp.concatenate(
                [bias_fn(2 * g + e, k0, tq), bias_fn(2 * g + e, s_real, META_ROWS)], axis=1)
            carry.append(_softmax_step(qs[2 * g + e], kk, vv, mask, bias, None, shift))

    def body(kc, carry):
        kb = pl.multiple_of(kc * tq, tq)
        out = []
        for g in range(n_groups):
            cols = slice(g * 128, (g + 1) * 128)
            kblk = k_ref[0, pl.ds(kb, tq), cols]
            vblk = v_ref[0, pl.ds(kb, tq), cols]
            for e in range(2):
                bias = None if bias_fn is None else bias_fn(2 * g + e, kb, tq)
                out.append(_softmax_step(qs[2 * g + e], kblk, vblk, None, bias, carry[2 * g + e], shift))
        return tuple(out)

    carry = lax.fori_loop(0, i, body, tuple(carry))
    if shift is None:
        return [acc / l for (_, l, acc) in carry]
    return [acc / jnp.sum(part, axis=1, keepdims=True) for (part, acc) in carry]


def _with_score_bound(bound_ref, run):
    bound = bound_ref[0]
    small = bound <= MAX_FIXED_SHIFT
    pl.when(small)(lambda: run(bound))
    pl.when(jnp.logical_not(small))(lambda: run(None))


def _diff_attn_kernel(bound_ref, q_ref, k_ref, v_ref, lam_ref, sg_ref, o_ref, *, tq, s_real, lam_init):
    def run(shift):
        outs = _attention_sweep(q_ref, k_ref, v_ref, pl.program_id(1), shift, tq=tq, s_real=s_real,
                                n_groups=DIFF_HEADS, diag_limit_fn=lambda r: (r // CHUNK + 1) * CHUNK, bias_fn=None)
        lam = lam_ref[...]
        lam = (jnp.exp(jnp.sum(lam[0:1] * lam[1:2], axis=1, keepdims=True))
               - jnp.exp(jnp.sum(lam[2:3] * lam[3:4], axis=1, keepdims=True)) + lam_init)
        for h in range(DIFF_HEADS):
            od = outs[2 * h] - lam * outs[2 * h + 1]
            y = od * lax.rsqrt(jnp.mean(od * od, axis=-1, keepdims=True) + EPS) * sg_ref[...]
            o_ref[0, :, h * 128:(h + 1) * 128] = (y * (1.0 - lam_init)).astype(BF16)

    _with_score_bound(bound_ref, run)


def _attention_call(kernel_fn, name, bound, q, k, v, extra, extra_specs, *, s_real):
    b, t_int, w = q.shape
    tq = min(256, s_real)
    return pl.pallas_call(
        functools.partial(kernel_fn, tq=tq, s_real=s_real),
        grid=(b, s_real // tq),
        in_specs=[pl.BlockSpec(memory_space=pltpu.SMEM),
                  pl.BlockSpec((1, tq, w), lambda bi, i: (bi, i, 0)),
                  pl.BlockSpec((1, t_int, w), lambda bi, i: (bi, 0, 0)),
                  pl.BlockSpec((1, t_int, w), lambda bi, i: (bi, 0, 0))] + extra_specs(tq, t_int),
        out_specs=pl.BlockSpec((1, tq, w), lambda bi, i: (bi, i, 0)),
        out_shape=jax.ShapeDtypeStruct((b, s_real, w), BF16),
        compiler_params=pltpu.CompilerParams(dimension_semantics=("arbitrary",) * 2,
                                             vmem_limit_bytes=VMEM_LIMIT_BYTES),
        name=name,
    )(bound, q, k, v, *extra)


def _score_bound(gq, gk):
    return (1.01 * LOG2E * HEAD_DIM ** 0.5 * jnp.max(jnp.abs(gq)) * jnp.max(jnp.abs(gk))).astype(F32).reshape(1)


def _diff_attention(bound, q, k, v, lam, sg, *, s_real, lam_init):
    specs = lambda tq, t_int: [pl.BlockSpec((4, HEAD_DIM), lambda bi, i: (0, 0)),
                               pl.BlockSpec((1, 128), lambda bi, i: (0, 0))]
    return _attention_call(functools.partial(_diff_attn_kernel, lam_init=lam_init), "diff_attn", bound, q, k, v,
                           (lam, sg), specs, s_real=s_real)


def _fox_attn_kernel(bound_ref, q_ref, k_ref, v_ref, col_ref, row_ref, o_ref, *, tq, s_real):
    def run(shift):
        cum = col_ref[0]

        def bias_fn(head, k0, width):
            return cum[:, head:head + 1] - row_ref[0, head, :, pl.ds(k0, width)]

        outs = _attention_sweep(q_ref, k_ref, v_ref, pl.program_id(1), shift, tq=tq, s_real=s_real,
                                n_groups=FOX_HEADS // 2, diag_limit_fn=lambda r: r + 1, bias_fn=bias_fn)
        lo = lax.broadcasted_iota(jnp.int32, (tq, 128), 1) < HEAD_DIM
        for g in range(FOX_HEADS // 2):
            o_ref[0, :, g * 128:(g + 1) * 128] = jnp.where(lo, outs[2 * g], outs[2 * g + 1]).astype(BF16)

    _with_score_bound(bound_ref, run)


def _fox_attention(bound, q, k, v, cum_col, cum_row, *, s_real):
    specs = lambda tq, t_int: [pl.BlockSpec((1, tq, 128), lambda bi, i: (bi, i, 0)),
                               pl.BlockSpec((1, FOX_HEADS, 1, t_int), lambda bi, i: (bi, 0, 0, 0))]
    return _attention_call(_fox_attn_kernel, "fox_attn", bound, q, k, v, (cum_col, cum_row), specs, s_real=s_real)


def _merge_kernel(ya_ref, yb_ref, ga_ref, gb_ref, x_ref, wd_ref, wf_ref, wo_ref, g_ref, hs_ref, h2_ref):
    merged = (ga_ref[0].astype(F32) * jnp.dot(ya_ref[0], wd_ref[...], preferred_element_type=F32)
              + gb_ref[0].astype(F32) * jnp.dot(yb_ref[0], wf_ref[...], preferred_element_type=F32))
    hs = x_ref[0] + jnp.dot(merged.astype(BF16), wo_ref[...], preferred_element_type=F32)
    hs_ref[0] = hs
    h2_ref[0] = (hs * lax.rsqrt(jnp.mean(hs * hs, axis=-1, keepdims=True) + EPS) * g_ref[...]).astype(BF16)


def _merge(ya, yb, ga, gb, x, wd, wf, wo, g):
    b, s_real, _ = x.shape
    tm = min(512, s_real)
    blk = lambda w: pl.BlockSpec((1, tm, w), lambda bi, i: (bi, i, 0))
    const = lambda shape: pl.BlockSpec(shape, lambda bi, i: (0, 0))
    return pl.pallas_call(
        _merge_kernel,
        grid=(b, s_real // tm),
        in_specs=[blk(BRANCH_WIDTH), blk(BRANCH_WIDTH), blk(D_MODEL), blk(D_MODEL), blk(D_MODEL),
                  const((BRANCH_WIDTH, D_MODEL)), const((BRANCH_WIDTH, D_MODEL)), const((D_MODEL, D_MODEL)),
                  const((1, D_MODEL))],
        out_specs=[blk(D_MODEL), blk(D_MODEL)],
        out_shape=[jax.ShapeDtypeStruct((b, s_real, D_MODEL), F32), jax.ShapeDtypeStruct((b, s_real, D_MODEL), BF16)],
        compiler_params=pltpu.CompilerParams(dimension_semantics=("arbitrary",) * 2,
                                             vmem_limit_bytes=VMEM_LIMIT_BYTES),
        name="merge",
    )(ya, yb, ga, gb, x, wd, wf, wo, g)


def _route_kernel(h2_ref, wq_ref, sk_ref, e1_ref, jr_ref, e2_ref, rk2_ref, st_ref, val_ref, rank_ref):
    n_lists = 2 * PEER_HEADS
    tn = h2_ref.shape[0]
    q = jnp.dot(h2_ref[...], wq_ref[...], preferred_element_type=F32).astype(BF16)
    for li in range(n_lists):
        st_ref[li] = _nt_dot(sk_ref[li], q[:, li * PEER_HALF:(li + 1) * PEER_HALF])

    def extract(s, index, index_end, on_hit, tie_exact):
        for it in range(PEER_TOPK):
            m = jnp.max(s, axis=0, keepdims=True)
            hit = s == m
            if tie_exact:
                hit = index == jnp.min(jnp.where(hit, index, index_end), axis=0, keepdims=True)
            on_hit(it, m, hit)
            s = jnp.where(hit, -jnp.inf, s)

    def any_tie(removed):
        return jnp.max(jnp.sum(removed, axis=0, keepdims=True)) > float(PEER_TOPK)

    rows = lax.broadcasted_iota(jnp.int32, (PEER_KEYS, tn), 0).astype(F32)

    def list_body(li, _):
        def run(tie_exact):
            state = {"rank": jnp.full((PEER_KEYS, tn), float(PEER_TOPK), F32), "vals": []}

            def on_hit(it, m, hit):
                state["rank"] = jnp.where(hit, float(it), state["rank"])
                state["vals"].append(m)

            extract(st_ref[li], rows, float(PEER_KEYS), on_hit, tie_exact)
            val_ref[li] = jnp.concatenate(state["vals"], axis=0)
            rank_ref[li] = state["rank"]
            return state["rank"]

        rank = run(False)

        @pl.when(any_tie(jnp.where(rank < float(PEER_TOPK), 1.0, 0.0)))
        def _():
            run(True)

        return 0

    lax.fori_loop(0, n_lists, list_body, 0)

    n_cand = PEER_TOPK * PEER_TOPK
    half = PEER_TOPK // 2
    iota8 = lax.broadcasted_iota(jnp.int32, (half, tn), 0)
    pos = jnp.concatenate(
        [lax.broadcasted_iota(jnp.int32, (PEER_TOPK, tn), 0)]
        + [iota8 + PEER_TOPK * r for r in range(1, half)]
        + [(iota8 + half) * PEER_TOPK], axis=0).astype(F32)

    def head_body(hd, _):
        a = val_ref[2 * hd]
        b = val_ref[2 * hd + 1]
        cand = jnp.concatenate(
            [a[0:1, :] + b] + [a[r:r + 1, :] + b[0:half, :] for r in range(1, half)]
            + [a[half:PEER_TOPK, :] + b[0:1, :]], axis=0)
        m0 = a[0:1, :] + b[0:1, :]

        def run(tie_exact):
            state = {"sel": jnp.zeros(cand.shape, F32), "z": jnp.zeros((1, tn), F32)}

            def on_hit(it, m, hit):
                state["sel"] = jnp.where(hit, 1.0, state["sel"])
                state["z"] = state["z"] + jnp.exp(m - m0)

            extract(cand, pos, float(n_cand), on_hit, tie_exact)
            sel = state["sel"]
            rank1 = rank_ref[2 * hd]
            jr = jnp.zeros(rank1.shape, F32)
            for r in range(PEER_TOPK):
                if r == 0:
                    j_r = jnp.sum(sel[0:PEER_TOPK, :], axis=0, keepdims=True)
                elif r < half:
                    j_r = jnp.sum(sel[PEER_TOPK + half * (r - 1):PEER_TOPK + half * r, :], axis=0, keepdims=True)
                else:
                    lo = PEER_TOPK + half * (half - 1) + (r - half)
                    j_r = sel[lo:lo + 1, :]
                jr = jnp.where(rank1 == float(r), j_r, jr)
            jr_ref[hd] = jr
            e1_ref[hd] = jnp.exp(st_ref[2 * hd] - a[0:1, :]) / state["z"]
            return sel

        sel = run(False)

        @pl.when(any_tie(sel))
        def _():
            run(True)

        e2_ref[hd] = jnp.exp(st_ref[2 * hd + 1] - b[0:1, :]).astype(BF16)
        rk2_ref[hd] = rank_ref[2 * hd + 1].astype(BF16)
        return 0

    lax.fori_loop(0, PEER_HEADS, head_body, 0)


def _route(h2, wq, sk):
    n = h2.shape[0]
    tn = _largest_divisor(n, 512, 128)
    tok = lambda i: (0, 0, i)
    tab = lambda dt: jax.ShapeDtypeStruct((PEER_HEADS, PEER_KEYS, n), dt)
    return pl.pallas_call(
        _route_kernel,
        grid=(n // tn,),
        in_specs=[pl.BlockSpec((tn, D_MODEL), lambda i: (i, 0)),
                  pl.BlockSpec((D_MODEL, 2 * PEER_HEADS * PEER_HALF), lambda i: (0, 0)),
                  pl.BlockSpec((2 * PEER_HEADS, PEER_KEYS, PEER_HALF), lambda i: (0, 0, 0))],
        out_specs=[pl.BlockSpec((PEER_HEADS, PEER_KEYS, tn), tok)] * 4,
        out_shape=[tab(F32), tab(F32), tab(BF16), tab(BF16)],
        scratch_shapes=[pltpu.VMEM((2 * PEER_HEADS, PEER_KEYS, tn), F32),
                        pltpu.VMEM((2 * PEER_HEADS, PEER_TOPK, tn), F32),
                        pltpu.VMEM((2 * PEER_HEADS, PEER_KEYS, tn), F32)],
        compiler_params=pltpu.CompilerParams(dimension_semantics=("arbitrary",),
                                             vmem_limit_bytes=VMEM_LIMIT_BYTES),
        name="route",
    )(h2, wq, sk)


def _peer_kernel(h2_ref, u_ref, vt_ref, e1_ref, jr_ref, e2_ref, rk2_ref, hs_ref, o_ref, acc_ref, *, rows_per_step):
    c = pl.program_id(1)
    tn = h2_ref.shape[0]

    @pl.when(c == 0)
    def _():
        acc_ref[...] = jnp.zeros_like(acc_ref)

    s_t = _nt_dot(u_ref[...], h2_ref[...])
    act = (0.5 * s_t * (1.0 + lax.erf(s_t * (2.0 ** -0.5)))).astype(BF16)
    shape = (PEER_KEYS, tn)
    rows16 = lambda row: jnp.concatenate([jnp.broadcast_to(row, (16, tn)).astype(BF16)] * (PEER_KEYS // 16), axis=0)
    ws = []
    for rr in range(rows_per_step):
        gate = jnp.zeros(shape, BF16)
        for hd in range(PEER_HEADS):
            gate = gate + jnp.where(rk2_ref[hd] < rows16(jr_ref[hd, rr:rr + 1, :]),
                                    e2_ref[hd] * rows16(e1_ref[hd, rr:rr + 1, :]), jnp.zeros(shape, BF16))
        ws.append(act[rr * PEER_KEYS:(rr + 1) * PEER_KEYS, :] * gate)
    w_t = jnp.concatenate(ws, axis=0)
    acc_ref[...] += jnp.dot(vt_ref[...], w_t, preferred_element_type=F32)

    @pl.when(c == pl.num_programs(1) - 1)
    def _():
        o_ref[...] = hs_ref[...] + acc_ref[...].T


def _peer(h2, u, vt, e1, jr, e2, rk2, hs):
    n = h2.shape[0]
    tn = _largest_divisor(n, 1024, 128)
    rows_per_step = 8
    ec = rows_per_step * PEER_KEYS
    n_chunks = PEER_KEYS // rows_per_step
    tab = pl.BlockSpec((PEER_HEADS, PEER_KEYS, tn), lambda i, c: (0, 0, i))
    row_tab = pl.BlockSpec((PEER_HEADS, rows_per_step, tn), lambda i, c: (0, c, i))
    return pl.pallas_call(
        functools.partial(_peer_kernel, rows_per_step=rows_per_step),
        grid=(n // tn, n_chunks),
        in_specs=[pl.BlockSpec((tn, D_MODEL), lambda i, c: (i, 0)),
                  pl.BlockSpec((ec, D_MODEL), lambda i, c: (c, 0)),
                  pl.BlockSpec((D_MODEL, ec), lambda i, c: (0, c)),
                  row_tab, row_tab, tab, tab,
                  pl.BlockSpec((tn, D_MODEL), lambda i, c: (i, 0))],
        out_specs=pl.BlockSpec((tn, D_MODEL), lambda i, c: (i, 0)),
        out_shape=jax.ShapeDtypeStruct((n, D_MODEL), F32),
        scratch_shapes=[pltpu.VMEM((D_MODEL, tn), F32)],
        compiler_params=pltpu.CompilerParams(dimension_semantics=("arbitrary", "arbitrary"),
                                             vmem_limit_bytes=VMEM_LIMIT_BYTES),
        name="peer",
    )(h2, u, vt, e1, jr, e2, rk2, hs)


def _relayout_w_in(w):
    hq = lambda lo: (w[:, lo:lo + 512].reshape(D_MODEL, 2, DIFF_HEADS, HEAD_DIM)
                     .transpose(0, 2, 1, 3).reshape(D_MODEL, 512))
    ff = jnp.pad(w[:, 3072:3080], ((0, 0), (0, 120)))
    return jnp.concatenate([hq(0), hq(512), w[:, 1024:3072], ff, w[:, 3080:5128]], axis=1).astype(BF16)


def kernel(x, meta_tokens, norm_mix_g, w_in, b_gate, b_forget, diff_qnorm_g, diff_knorm_g, diff_lambda,
           diff_subln_g, fox_qnorm_g, fox_knorm_g, w_branch_diff, w_branch_fox, w_out, norm_ffn_g, peer_w_q,
           peer_subkeys, peer_u, peer_v):
    b, s_real, d = x.shape
    assert d == D_MODEL and s_real % 128 == 0 and norm_mix_g.shape[0] == 1
    t_int = s_real + META_ROWS
    layer = 0
    lam_init = 0.8 - 0.6 * math.exp(-0.3 * layer)

    meta_blk = jnp.pad(meta_tokens.astype(F32), ((0, META_ROWS - N_META), (0, 0)))
    hin = jnp.concatenate([x, jnp.broadcast_to(meta_blk[None], (b, META_ROWS, d))], axis=1).reshape(b * t_int, d)

    rows = jnp.arange(t_int, dtype=jnp.int32)
    pos = jnp.where(rows < s_real, rows + N_META, rows - s_real).astype(F32)
    inv_freq = ROPE_THETA ** (-jnp.arange(0, HEAD_DIM, 2, dtype=F32) / HEAD_DIM)
    ang = pos[:, None] * inv_freq[None, :]
    cos = jnp.tile(jnp.cos(ang), (1, 4))
    sin = jnp.tile(jnp.sin(ang), (1, 4))

    tile8 = lambda g: jnp.tile(g.astype(F32), 8)[None, :]
    gidx = jnp.arange(BRANCH_WIDTH) // HEAD_DIM
    gmat = (gidx[:, None] == gidx[None, :]).astype(BF16)
    bf = jnp.pad(b_forget[layer].astype(F32), (0, 120))[None, :]

    dq, dk, dv, fq, fk, fv, lf, ga, gb = _proj(
        hin, norm_mix_g[layer][None, :], _relayout_w_in(w_in[layer]), cos, sin,
        tile8(diff_qnorm_g[layer]), tile8(diff_knorm_g[layer]), tile8(fox_qnorm_g[layer]), tile8(fox_knorm_g[layer]),
        bf, b_gate[layer][None, :], gmat, t_int=t_int, n_valid=s_real + N_META)

    r3 = lambda a: a.reshape(b, t_int, a.shape[-1])
    tri = (jnp.arange(128)[:, None] >= jnp.arange(128)[None, :]).astype(BF16)
    cum_col, cum_row = _cum(r3(lf), tri)

    ya = _diff_attention(_score_bound(diff_qnorm_g[layer], diff_knorm_g[layer]), r3(dq), r3(dk), r3(dv),
                         diff_lambda[layer].astype(F32), diff_subln_g[layer].astype(F32)[None, :],
                         s_real=s_real, lam_init=lam_init)
    yb = _fox_attention(_score_bound(fox_qnorm_g[layer], fox_knorm_g[layer]), r3(fq), r3(fk), r3(fv),
                        cum_col, cum_row[:, :, None, :], s_real=s_real)

    hs, h2 = _merge(ya, yb, r3(ga), r3(gb), x, w_branch_diff[layer].astype(BF16), w_branch_fox[layer].astype(BF16),
                    w_out[layer].astype(BF16), norm_ffn_g[layer][None, :])

    n = b * s_real
    h2 = h2.reshape(n, d)
    sk = peer_subkeys[layer].astype(BF16).reshape(2 * PEER_HEADS, PEER_KEYS, PEER_HALF)
    e1, jr, e2, rk2 = _route(h2, peer_w_q[layer].astype(BF16), sk)
    out = _peer(h2, peer_u[layer].astype(BF16), peer_v[layer].astype(BF16).T, e1, jr, e2, rk2, hs.reshape(n, d))
    return out.reshape(b, s_real, d)
```

```python
import functools
import math

import jax
import jax.numpy as jnp
from jax import lax
from jax.experimental import pallas as pl
from jax.experimental.pallas import tpu as pltpu

F32 = jnp.float32
BF16 = jnp.bfloat16

D_MODEL = 1024
N_META = 16
META_ROWS = 128
CHUNK = 64
ROPE_THETA = 10000.0
EPS = 1e-6
NEG = -1e30
LOG2E = math.log2(math.e)
MAX_FIXED_SHIFT = 60.0

DIFF_HEADS = 4
HEAD_DIM = 64
FOX_HEADS = 8
BRANCH_WIDTH = 512

PEER_HEADS = 8
PEER_KEYS = 128
PEER_HALF = 128
PEER_TOPK = 16

_O_DQ, _O_DK, _O_DV, _O_FQ, _O_FK, _O_FV, _O_FF, _O_GA, _O_GB, _O_END = (
    0, 512, 1024, 1536, 2048, 2560, 3072, 3200, 4224, 5248)

VMEM_LIMIT_BYTES = 56 * 1024 * 1024


def _nt_dot(a, b):
    return lax.dot_general(a, b, (((1,), (1,)), ((), ())), preferred_element_type=F32)


def _largest_divisor(n, cap, mult):
    best = None
    for t in range(mult, min(n, cap) + 1, mult):
        if n % t == 0:
            best = t
    assert best is not None, (n, cap, mult)
    return best


def _group_mean_sq(y, gmat):
    v = y * y
    hi = v.astype(BF16)
    lo = (v - hi.astype(F32)).astype(BF16)
    return (jnp.dot(hi, gmat, preferred_element_type=F32)
            + jnp.dot(lo, gmat, preferred_element_type=F32)) * (1.0 / HEAD_DIM)


def _rope(y, cos, sin):
    lane = lax.broadcasted_iota(jnp.int32, y.shape, 1)
    first_half = (lane & (HEAD_DIM - 1)) < (HEAD_DIM // 2)
    w = y.shape[1]
    rot = jnp.where(first_half, -pltpu.roll(y, w - HEAD_DIM // 2, 1), pltpu.roll(y, HEAD_DIM // 2, 1))
    return y * cos + rot * sin


def _proj_kernel(x_ref, g_ref, w_ref, cos_ref, sin_ref, gq_ref, gk_ref, fgq_ref, fgk_ref, bf_ref, bg_ref,
                 gmat_ref, dq_ref, dk_ref, dv_ref, fq_ref, fk_ref, fv_ref, lf_ref, ga_ref, gb_ref,
                 *, tm, blocks_per_batch, n_valid):
    x = x_ref[...]
    h = (x * lax.rsqrt(jnp.mean(x * x, axis=-1, keepdims=True) + EPS) * g_ref[...]).astype(BF16)
    gmat = gmat_ref[...]
    cos = jnp.concatenate([cos_ref[...]] * 4, axis=1)
    sin = jnp.concatenate([sin_ref[...]] * 4, axis=1)

    def sec(lo, hi):
        return jnp.dot(h, w_ref[:, lo:hi], preferred_element_type=F32)

    def headnorm(y, gain):
        return y * lax.rsqrt(_group_mean_sq(y, gmat) + EPS) * gain

    scale = HEAD_DIM ** -0.5 * LOG2E
    dq_ref[...] = (_rope(headnorm(sec(_O_DQ, _O_DK), gq_ref[...]), cos, sin) * scale).astype(BF16)
    dk_ref[...] = _rope(headnorm(sec(_O_DK, _O_DV), gk_ref[...]), cos, sin).astype(BF16)
    dv_ref[...] = sec(_O_DV, _O_FQ).astype(BF16)
    fq_ref[...] = (headnorm(sec(_O_FQ, _O_FK), fgq_ref[...]) * scale).astype(BF16)
    fk_ref[...] = headnorm(sec(_O_FK, _O_FV), fgk_ref[...]).astype(BF16)
    fv_ref[...] = sec(_O_FV, _O_FF).astype(BF16)

    z = sec(_O_FF, _O_GA) + bf_ref[...]
    log_f = jnp.minimum(z, 0.0) - jnp.log1p(jnp.exp(-jnp.abs(z)))
    row = (pl.program_id(0) % blocks_per_batch) * tm + lax.broadcasted_iota(jnp.int32, z.shape, 0)
    lf_ref[...] = jnp.where(row < n_valid, log_f, 0.0)

    ga_ref[...] = jax.nn.sigmoid(sec(_O_GA, _O_GB) + bg_ref[:, :D_MODEL]).astype(BF16)
    gb_ref[...] = jax.nn.sigmoid(sec(_O_GB, _O_END) + bg_ref[:, D_MODEL:]).astype(BF16)


def _proj(hin, g, w_all, cos, sin, gq, gk, fgq, fgk, bf, bg, gmat, *, t_int, n_valid):
    n = hin.shape[0]
    tm = _largest_divisor(t_int, 544, 16)
    bpb = t_int // tm
    row = lambda i: (i, 0)
    const = lambda i: (0, 0)
    tab = lambda i: (i % bpb, 0)
    wide = lambda w: pl.BlockSpec((tm, w), row)
    out_shapes = [jax.ShapeDtypeStruct((n, BRANCH_WIDTH), BF16)] * 6 + [
        jax.ShapeDtypeStruct((n, 128), F32),
        jax.ShapeDtypeStruct((n, D_MODEL), BF16), jax.ShapeDtypeStruct((n, D_MODEL), BF16)]
    return pl.pallas_call(
        functools.partial(_proj_kernel, tm=tm, blocks_per_batch=bpb, n_valid=n_valid),
        grid=(n // tm,),
        in_specs=[wide(D_MODEL), pl.BlockSpec((1, D_MODEL), const),
                  pl.BlockSpec((D_MODEL, _O_END), const, pipeline_mode=pl.Buffered(1)),
                  pl.BlockSpec((tm, 128), tab), pl.BlockSpec((tm, 128), tab),
                  pl.BlockSpec((1, BRANCH_WIDTH), const), pl.BlockSpec((1, BRANCH_WIDTH), const),
                  pl.BlockSpec((1, BRANCH_WIDTH), const), pl.BlockSpec((1, BRANCH_WIDTH), const),
                  pl.BlockSpec((1, 128), const), pl.BlockSpec((1, 2 * D_MODEL), const),
                  pl.BlockSpec((BRANCH_WIDTH, BRANCH_WIDTH), const)],
        out_specs=[wide(BRANCH_WIDTH)] * 6 + [wide(128), wide(D_MODEL), wide(D_MODEL)],
        out_shape=out_shapes,
        compiler_params=pltpu.CompilerParams(dimension_semantics=("arbitrary",),
                                             vmem_limit_bytes=VMEM_LIMIT_BYTES),
        name="proj",
    )(hin, g, w_all, cos, sin, gq, gk, fgq, fgk, bf, bg, gmat)


def _cum_kernel(lf_ref, tri_ref, col_ref, row_ref, *, n_blocks):
    tri = tri_ref[...]
    carry = jnp.zeros((1, 128), F32)
    for blk in [n_blocks - 1] + list(range(n_blocks - 1)):
        sl = slice(blk * 128, (blk + 1) * 128)
        v = lf_ref[0, sl, :] * LOG2E
        h1 = v.astype(BF16)
        r1 = v - h1.astype(F32)
        h2 = r1.astype(BF16)
        h3 = (r1 - h2.astype(F32)).astype(BF16)
        c = (jnp.dot(tri, h1, preferred_element_type=F32) + jnp.dot(tri, h2, preferred_element_type=F32)
             + jnp.dot(tri, h3, preferred_element_type=F32)) + carry
        col_ref[0, sl, :] = c
        row_ref[0, :, sl] = c.T[0:FOX_HEADS, :]
        carry = c[127:128, :]


def _cum(lf3, tri):
    b, t_int, _ = lf3.shape
    return pl.pallas_call(
        functools.partial(_cum_kernel, n_blocks=t_int // 128),
        grid=(b,),
        in_specs=[pl.BlockSpec((1, t_int, 128), lambda i: (i, 0, 0)), pl.BlockSpec((128, 128), lambda i: (0, 0))],
        out_specs=[pl.BlockSpec((1, t_int, 128), lambda i: (i, 0, 0)),
                   pl.BlockSpec((1, FOX_HEADS, t_int), lambda i: (i, 0, 0))],
        out_shape=[jax.ShapeDtypeStruct((b, t_int, 128), F32), jax.ShapeDtypeStruct((b, FOX_HEADS, t_int), F32)],
        compiler_params=pltpu.CompilerParams(dimension_semantics=("arbitrary",)),
        name="cum",
    )(lf3, tri)


def _softmax_step(qg, kblk, vblk, mask, bias, state, shift):
    s = _nt_dot(qg, kblk)
    if bias is not None:
        s = s + bias
    if mask is not None:
        s = jnp.where(mask, s, NEG)
    if shift is not None:
        p = jnp.exp2(s - shift)
        part = p[:, 0:128]
        for j in range(1, p.shape[1] // 128):
            part = part + p[:, j * 128:(j + 1) * 128]
        pv = jnp.dot(p.astype(BF16), vblk, preferred_element_type=F32)
        return (part, pv) if state is None else (state[0] + part, state[1] + pv)
    blk_max = jnp.max(s, axis=1, keepdims=True)
    if state is None:
        p = jnp.exp2(s - blk_max)
        return (blk_max, jnp.sum(p, axis=1, keepdims=True),
                jnp.dot(p.astype(BF16), vblk, preferred_element_type=F32))
    m_old, l, acc = state
    m = jnp.maximum(m_old, blk_max)
    p = jnp.exp2(s - m)
    alpha = jnp.exp2(m_old - m)
    return (m, alpha * l + jnp.sum(p, axis=1, keepdims=True),
            alpha * acc + jnp.dot(p.astype(BF16), vblk, preferred_element_type=F32))


def _split_halves(q_ref, n_groups):
    qs = []
    for g in range(n_groups):
        q = q_ref[0, :, g * 128:(g + 1) * 128]
        lo = lax.broadcasted_iota(jnp.int32, q.shape, 1) < HEAD_DIM
        qs += [jnp.where(lo, q, jnp.zeros_like(q)), jnp.where(lo, jnp.zeros_like(q), q)]
    return qs


def _attention_sweep(q_ref, k_ref, v_ref, i, shift, *, tq, s_real, n_groups, diag_limit_fn, bias_fn):
    qs = _split_halves(q_ref, n_groups)
    k0 = pl.multiple_of(i * tq, tq)
    r = lax.broadcasted_iota(jnp.int32, (tq, tq + META_ROWS), 0)
    c = lax.broadcasted_iota(jnp.int32, (tq, tq + META_ROWS), 1)
    mask = c < jnp.where(c < tq, diag_limit_fn(r), tq + N_META)
    carry = []
    for g in range(n_groups):
        cols = slice(g * 128, (g + 1) * 128)
        kk = jnp.concatenate([k_ref[0, pl.ds(k0, tq), cols], k_ref[0, s_real:s_real + META_ROWS, cols]], axis=0)
        vv = jnp.concatenate([v_ref[0, pl.ds(k0, tq), cols], v_ref[0, s_real:s_real + META_ROWS, cols]], axis=0)
        for e in range(2):
            bias = None if bias_fn is None else jnp.concatenate(
                [bias_fn(2 * g + e, k0, tq), bias_fn(2 * g + e, s_real, META_ROWS)], axis=1)
            carry.append(_softmax_step(qs[2 * g + e], kk, vv, mask, bias, None, shift))

    def body(kc, carry):
        kb = pl.multiple_of(kc * tq, tq)
        out = []
        for g in range(n_groups):
            cols = slice(g * 128, (g + 1) * 128)
            kblk = k_ref[0, pl.ds(kb, tq), cols]
            vblk = v_ref[0, pl.ds(kb, tq), cols]
            for e in range(2):
                bias = None if bias_fn is None else bias_fn(2 * g + e, kb, tq)
                out.append(_softmax_step(qs[2 * g + e], kblk, vblk, None, bias, carry[2 * g + e], shift))
        return tuple(out)

    carry = lax.fori_loop(0, i, body, tuple(carry))
    if shift is None:
        return [acc / l for (_, l, acc) in carry]
    return [acc / jnp.sum(part, axis=1, keepdims=True) for (part, acc) in carry]


def _with_score_bound(bound_ref, run):
    bound = bound_ref[0]
    small = bound <= MAX_FIXED_SHIFT
    pl.when(small)(lambda: run(bound))
    pl.when(jnp.logical_not(small))(lambda: run(None))


def _diff_attn_kernel(bound_ref, q_ref, k_ref, v_ref, lam_ref, sg_ref, o_ref, *, tq, s_real, lam_init):
    def run(shift):
        outs = _attention_sweep(q_ref, k_ref, v_ref, pl.program_id(1), shift, tq=tq, s_real=s_real,
                                n_groups=DIFF_HEADS, diag_limit_fn=lambda r: (r // CHUNK + 1) * CHUNK, bias_fn=None)
        lam = lam_ref[...]
        lam = (jnp.exp(jnp.sum(lam[0:1] * lam[1:2], axis=1, keepdims=True))
               - jnp.exp(jnp.sum(lam[2:3] * lam[3:4], axis=1, keepdims=True)) + lam_init)
        for h in range(DIFF_HEADS):
            od = outs[2 * h] - lam * outs[2 * h + 1]
            y = od * lax.rsqrt(jnp.mean(od * od, axis=-1, keepdims=True) + EPS) * sg_ref[...]
            o_ref[0, :, h * 128:(h + 1) * 128] = (y * (1.0 - lam_init)).astype(BF16)

    _with_score_bound(bound_ref, run)


def _attention_call(kernel_fn, name, bound, q, k, v, extra, extra_specs, *, s_real):
    b, t_int, w = q.shape
    tq = min(512, s_real)
    return pl.pallas_call(
        functools.partial(kernel_fn, tq=tq, s_real=s_real),
        grid=(b, s_real // tq),
        in_specs=[pl.BlockSpec(memory_space=pltpu.SMEM),
                  pl.BlockSpec((1, tq, w), lambda bi, i: (bi, i, 0)),
                  pl.BlockSpec((1, t_int, w), lambda bi, i: (bi, 0, 0)),
                  pl.BlockSpec((1, t_int, w), lambda bi, i: (bi, 0, 0))] + extra_specs(tq, t_int),
        out_specs=pl.BlockSpec((1, tq, w), lambda bi, i: (bi, i, 0)),
        out_shape=jax.ShapeDtypeStruct((b, s_real, w), BF16),
        compiler_params=pltpu.CompilerParams(dimension_semantics=("arbitrary",) * 2,
                                             vmem_limit_bytes=VMEM_LIMIT_BYTES),
        name=name,
    )(bound, q, k, v, *extra)


def _score_bound(gq, gk):
    return (1.01 * LOG2E * HEAD_DIM ** 0.5 * jnp.max(jnp.abs(gq)) * jnp.max(jnp.abs(gk))).astype(F32).reshape(1)


def _diff_attention(bound, q, k, v, lam, sg, *, s_real, lam_init):
    specs = lambda tq, t_int: [pl.BlockSpec((4, HEAD_DIM), lambda bi, i: (0, 0)),
                               pl.BlockSpec((1, 128), lambda bi, i: (0, 0))]
    return _attention_call(functools.partial(_diff_attn_kernel, lam_init=lam_init), "diff_attn", bound, q, k, v,
                           (lam, sg), specs, s_real=s_real)


def _fox_attn_kernel(bound_ref, q_ref, k_ref, v_ref, col_ref, row_ref, o_ref, *, tq, s_real):
    def run(shift):
        cum = col_ref[0]

        def bias_fn(head, k0, width):
            return cum[:, head:head + 1] - row_ref[0, head, :, pl.ds(k0, width)]

        outs = _attention_sweep(q_ref, k_ref, v_ref, pl.program_id(1), shift, tq=tq, s_real=s_real,
                                n_groups=FOX_HEADS // 2, diag_limit_fn=lambda r: r + 1, bias_fn=bias_fn)
        lo = lax.broadcasted_iota(jnp.int32, (tq, 128), 1) < HEAD_DIM
        for g in range(FOX_HEADS // 2):
            o_ref[0, :, g * 128:(g + 1) * 128] = jnp.where(lo, outs[2 * g], outs[2 * g + 1]).astype(BF16)

    _with_score_bound(bound_ref, run)


def _fox_attention(bound, q, k, v, cum_col, cum_row, *, s_real):
    specs = lambda tq, t_int: [pl.BlockSpec((1, tq, 128), lambda bi, i: (bi, i, 0)),
                               pl.BlockSpec((1, FOX_HEADS, 1, t_int), lambda bi, i: (bi, 0, 0, 0))]
    return _attention_call(_fox_attn_kernel, "fox_attn", bound, q, k, v, (cum_col, cum_row), specs, s_real=s_real)


def _merge_kernel(ya_ref, yb_ref, ga_ref, gb_ref, x_ref, wd_ref, wf_ref, wo_ref, g_ref, hs_ref, h2_ref):
    merged = (ga_ref[0].astype(F32) * jnp.dot(ya_ref[0], wd_ref[...], preferred_element_type=F32)
              + gb_ref[0].astype(F32) * jnp.dot(yb_ref[0], wf_ref[...], preferred_element_type=F32))
    hs = x_ref[0] + jnp.dot(merged.astype(BF16), wo_ref[...], preferred_element_type=F32)
    hs_ref[0] = hs
    h2_ref[0] = (hs * lax.rsqrt(jnp.mean(hs * hs, axis=-1, keepdims=True) + EPS) * g_ref[...]).astype(BF16)


def _merge(ya, yb, ga, gb, x, wd, wf, wo, g):
    b, s_real, _ = x.shape
    tm = min(512, s_real)
    blk = lambda w: pl.BlockSpec((1, tm, w), lambda bi, i: (bi, i, 0))
    const = lambda shape: pl.BlockSpec(shape, lambda bi, i: (0, 0))
    return pl.pallas_call(
        _merge_kernel,
        grid=(b, s_real // tm),
        in_specs=[blk(BRANCH_WIDTH), blk(BRANCH_WIDTH), blk(D_MODEL), blk(D_MODEL), blk(D_MODEL),
                  const((BRANCH_WIDTH, D_MODEL)), const((BRANCH_WIDTH, D_MODEL)), const((D_MODEL, D_MODEL)),
                  const((1, D_MODEL))],
        out_specs=[blk(D_MODEL), blk(D_MODEL)],
        out_shape=[jax.ShapeDtypeStruct((b, s_real, D_MODEL), F32), jax.ShapeDtypeStruct((b, s_real, D_MODEL), BF16)],
        compiler_params=pltpu.CompilerParams(dimension_semantics=("arbitrary",) * 2,
                                             vmem_limit_bytes=VMEM_LIMIT_BYTES),
        name="merge",
    )(ya, yb, ga, gb, x, wd, wf, wo, g)


def _route_kernel(h2_ref, wq_ref, sk_ref, e1_ref, jr_ref, e2_ref, rk2_ref, st_ref, val_ref, rank_ref):
    n_lists = 2 * PEER_HEADS
    tn = h2_ref.shape[0]
    q = jnp.dot(h2_ref[...], wq_ref[...], preferred_element_type=F32).astype(BF16)
    for li in range(n_lists):
        st_ref[li] = _nt_dot(sk_ref[li], q[:, li * PEER_HALF:(li + 1) * PEER_HALF])

    def extract(s, index, index_end, on_hit, tie_exact):
        for it in range(PEER_TOPK):
            m = jnp.max(s, axis=0, keepdims=True)
            hit = s == m
            if tie_exact:
                hit = index == jnp.min(jnp.where(hit, index, index_end), axis=0, keepdims=True)
            on_hit(it, m, hit)
            s = jnp.where(hit, -jnp.inf, s)

    def any_tie(removed):
        return jnp.max(jnp.sum(removed, axis=0, keepdims=True)) > float(PEER_TOPK)

    rows = lax.broadcasted_iota(jnp.int32, (PEER_KEYS, tn), 0).astype(F32)

    def list_pair_body(hd, _):
        def run(li, tie_exact):
            state = {"rank": jnp.full((PEER_KEYS, tn), float(PEER_TOPK), F32), "vals": []}

            def on_hit(it, m, hit):
                state["rank"] = jnp.where(hit, float(it), state["rank"])
                state["vals"].append(m)

            extract(st_ref[li], rows, float(PEER_KEYS), on_hit, tie_exact)
            val_ref[li] = jnp.concatenate(state["vals"], axis=0)
            rank_ref[li] = state["rank"]
            return state["rank"]

        ranks = [run(2 * hd + half_id, False) for half_id in range(2)]
        for half_id in range(2):
            @pl.when(any_tie(jnp.where(ranks[half_id] < float(PEER_TOPK), 1.0, 0.0)))
            def _():
                run(2 * hd + half_id, True)

        return 0

    lax.fori_loop(0, PEER_HEADS, list_pair_body, 0)

    n_cand = PEER_TOPK * PEER_TOPK
    half = PEER_TOPK // 2
    iota8 = lax.broadcasted_iota(jnp.int32, (half, tn), 0)
    pos = jnp.concatenate(
        [lax.broadcasted_iota(jnp.int32, (PEER_TOPK, tn), 0)]
        + [iota8 + PEER_TOPK * r for r in range(1, half)]
        + [(iota8 + half) * PEER_TOPK], axis=0).astype(F32)

    def head_body(hd, _):
        a = val_ref[2 * hd]
        b = val_ref[2 * hd + 1]
        cand = jnp.concatenate(
            [a[0:1, :] + b] + [a[r:r + 1, :] + b[0:half, :] for r in range(1, half)]
            + [a[half:PEER_TOPK, :] + b[0:1, :]], axis=0)
        m0 = a[0:1, :] + b[0:1, :]

        def run(tie_exact):
            state = {"sel": jnp.zeros(cand.shape, F32), "z": jnp.zeros((1, tn), F32)}

            def on_hit(it, m, hit):
                state["sel"] = jnp.where(hit, 1.0, state["sel"])
                state["z"] = state["z"] + jnp.exp(m - m0)

            extract(cand, pos, float(n_cand), on_hit, tie_exact)
            sel = state["sel"]
            rank1 = rank_ref[2 * hd]
            jr = jnp.zeros(rank1.shape, F32)
            for r in range(PEER_TOPK):
                if r == 0:
                    j_r = jnp.sum(sel[0:PEER_TOPK, :], axis=0, keepdims=True)
                elif r < half:
                    j_r = jnp.sum(sel[PEER_TOPK + half * (r - 1):PEER_TOPK + half * r, :], axis=0, keepdims=True)
                else:
                    lo = PEER_TOPK + half * (half - 1) + (r - half)
                    j_r = sel[lo:lo + 1, :]
                jr = jnp.where(rank1 == float(r), j_r, jr)
            jr_ref[hd] = jr
            e1_ref[hd] = jnp.exp(st_ref[2 * hd] - a[0:1, :]) / state["z"]
            return sel

        sel = run(False)

        @pl.when(any_tie(sel))
        def _():
            run(True)

        e2_ref[hd] = jnp.exp(st_ref[2 * hd + 1] - b[0:1, :]).astype(BF16)
        rk2_ref[hd] = rank_ref[2 * hd + 1].astype(BF16)
        return 0

    lax.fori_loop(0, PEER_HEADS, head_body, 0)


def _route(h2, wq, sk):
    n = h2.shape[0]
    tn = _largest_divisor(n, 512, 128)
    tok = lambda i: (0, 0, i)
    tab = lambda dt: jax.ShapeDtypeStruct((PEER_HEADS, PEER_KEYS, n), dt)
    return pl.pallas_call(
        _route_kernel,
        grid=(n // tn,),
        in_specs=[pl.BlockSpec((tn, D_MODEL), lambda i: (i, 0)),
                  pl.BlockSpec((D_MODEL, 2 * PEER_HEADS * PEER_HALF), lambda i: (0, 0)),
                  pl.BlockSpec((2 * PEER_HEADS, PEER_KEYS, PEER_HALF), lambda i: (0, 0, 0))],
        out_specs=[pl.BlockSpec((PEER_HEADS, PEER_KEYS, tn), tok)] * 4,
        out_shape=[tab(F32), tab(F32), tab(BF16), tab(BF16)],
        scratch_shapes=[pltpu.VMEM((2 * PEER_HEADS, PEER_KEYS, tn), F32),
                        pltpu.VMEM((2 * PEER_HEADS, PEER_TOPK, tn), F32),
                        pltpu.VMEM((2 * PEER_HEADS, PEER_KEYS, tn), F32)],
        compiler_params=pltpu.CompilerParams(dimension_semantics=("arbitrary",),
                                             vmem_limit_bytes=VMEM_LIMIT_BYTES),
        name="route",
    )(h2, wq, sk)


def _peer_kernel(h2_ref, u_ref, vt_ref, e1_ref, jr_ref, e2_ref, rk2_ref, hs_ref, o_ref, acc_ref, *, rows_per_step):
    c = pl.program_id(1)
    tn = h2_ref.shape[0]

    @pl.when(c == 0)
    def _():
        acc_ref[...] = jnp.zeros_like(acc_ref)

    s_t = _nt_dot(u_ref[...], h2_ref[...])
    act = (0.5 * s_t * (1.0 + lax.erf(s_t * (2.0 ** -0.5)))).astype(BF16)
    shape = (PEER_KEYS, tn)
    rows16 = lambda row: jnp.concatenate([jnp.broadcast_to(row, (16, tn)).astype(BF16)] * (PEER_KEYS // 16), axis=0)
    ws = []
    for rr in range(rows_per_step):
        gate = jnp.zeros(shape, BF16)
        for hd in range(PEER_HEADS):
            gate = gate + jnp.where(rk2_ref[hd] < rows16(jr_ref[hd, rr:rr + 1, :]),
                                    e2_ref[hd] * rows16(e1_ref[hd, rr:rr + 1, :]), jnp.zeros(shape, BF16))
        ws.append(act[rr * PEER_KEYS:(rr + 1) * PEER_KEYS, :] * gate)
    w_t = jnp.concatenate(ws, axis=0)
    acc_ref[...] += jnp.dot(vt_ref[...], w_t, preferred_element_type=F32)

    @pl.when(c == pl.num_programs(1) - 1)
    def _():
        o_ref[...] = hs_ref[...] + acc_ref[...].T


def _peer(h2, u, vt, e1, jr, e2, rk2, hs):
    n = h2.shape[0]
    tn = _largest_divisor(n, 1024, 128)
    rows_per_step = 8
    ec = rows_per_step * PEER_KEYS
    n_chunks = PEER_KEYS // rows_per_step
    tab = pl.BlockSpec((PEER_HEADS, PEER_KEYS, tn), lambda i, c: (0, 0, i))
    row_tab = pl.BlockSpec((PEER_HEADS, rows_per_step, tn), lambda i, c: (0, c, i))
    return pl.pallas_call(
        functools.partial(_peer_kernel, rows_per_step=rows_per_step),
        grid=(n // tn, n_chunks),
        in_specs=[pl.BlockSpec((tn, D_MODEL), lambda i, c: (i, 0)),
                  pl.BlockSpec((ec, D_MODEL), lambda i, c: (c, 0)),
                  pl.BlockSpec((D_MODEL, ec), lambda i, c: (0, c)),
                  row_tab, row_tab, tab, tab,
                  pl.BlockSpec((tn, D_MODEL), lambda i, c: (i, 0))],
        out_specs=pl.BlockSpec((tn, D_MODEL), lambda i, c: (i, 0)),
        out_shape=jax.ShapeDtypeStruct((n, D_MODEL), F32),
        scratch_shapes=[pltpu.VMEM((D_MODEL, tn), F32)],
        compiler_params=pltpu.CompilerParams(dimension_semantics=("arbitrary", "arbitrary"),
                                             vmem_limit_bytes=VMEM_LIMIT_BYTES),
        name="peer",
    )(h2, u, vt, e1, jr, e2, rk2, hs)


def _relayout_w_in(w):
    hq = lambda lo: (w[:, lo:lo + 512].reshape(D_MODEL, 2, DIFF_HEADS, HEAD_DIM)
                     .transpose(0, 2, 1, 3).reshape(D_MODEL, 512))
    ff = jnp.pad(w[:, 3072:3080], ((0, 0), (0, 120)))
    return jnp.concatenate([hq(0), hq(512), w[:, 1024:3072], ff, w[:, 3080:5128]], axis=1).astype(BF16)


def kernel(x, meta_tokens, norm_mix_g, w_in, b_gate, b_forget, diff_qnorm_g, diff_knorm_g, diff_lambda,
           diff_subln_g, fox_qnorm_g, fox_knorm_g, w_branch_diff, w_branch_fox, w_out, norm_ffn_g, peer_w_q,
           peer_subkeys, peer_u, peer_v):
    b, s_real, d = x.shape
    assert d == D_MODEL and s_real % 128 == 0 and norm_mix_g.shape[0] == 1
    t_int = s_real + META_ROWS
    layer = 0
    lam_init = 0.8 - 0.6 * math.exp(-0.3 * layer)

    meta_blk = jnp.pad(meta_tokens.astype(F32), ((0, META_ROWS - N_META), (0, 0)))
    hin = jnp.concatenate([x, jnp.broadcast_to(meta_blk[None], (b, META_ROWS, d))], axis=1).reshape(b * t_int, d)

    rows = jnp.arange(t_int, dtype=jnp.int32)
    pos = jnp.where(rows < s_real, rows + N_META, rows - s_real).astype(F32)
    inv_freq = ROPE_THETA ** (-jnp.arange(0, HEAD_DIM, 2, dtype=F32) / HEAD_DIM)
    ang = pos[:, None] * inv_freq[None, :]
    cos = jnp.tile(jnp.cos(ang), (1, 4))
    sin = jnp.tile(jnp.sin(ang), (1, 4))

    tile8 = lambda g: jnp.tile(g.astype(F32), 8)[None, :]
    gidx = jnp.arange(BRANCH_WIDTH) // HEAD_DIM
    gmat = (gidx[:, None] == gidx[None, :]).astype(BF16)
    bf = jnp.pad(b_forget[layer].astype(F32), (0, 120))[None, :]

    dq, dk, dv, fq, fk, fv, lf, ga, gb = _proj(
        hin, norm_mix_g[layer][None, :], _relayout_w_in(w_in[layer]), cos, sin,
        tile8(diff_qnorm_g[layer]), tile8(diff_knorm_g[layer]), tile8(fox_qnorm_g[layer]), tile8(fox_knorm_g[layer]),
        bf, b_gate[layer][None, :], gmat, t_int=t_int, n_valid=s_real + N_META)

    r3 = lambda a: a.reshape(b, t_int, a.shape[-1])
    tri = (jnp.arange(128)[:, None] >= jnp.arange(128)[None, :]).astype(BF16)
    cum_col, cum_row = _cum(r3(lf), tri)

    ya = _diff_attention(_score_bound(diff_qnorm_g[layer], diff_knorm_g[layer]), r3(dq), r3(dk), r3(dv),
                         diff_lambda[layer].astype(F32), diff_subln_g[layer].astype(F32)[None, :],
                         s_real=s_real, lam_init=lam_init)
    yb = _fox_attention(_score_bound(fox_qnorm_g[layer], fox_knorm_g[layer]), r3(fq), r3(fk), r3(fv),
                        cum_col, cum_row[:, :, None, :], s_real=s_real)

    hs, h2 = _merge(ya, yb, r3(ga), r3(gb), x, w_branch_diff[layer].astype(BF16), w_branch_fox[layer].astype(BF16),
                    w_out[layer].astype(BF16), norm_ffn_g[layer][None, :])

    n = b * s_real
    h2 = h2.reshape(n, d)
    sk = peer_subkeys[layer].astype(BF16).reshape(2 * PEER_HEADS, PEER_KEYS, PEER_HALF)
    e1, jr, e2, rk2 = _route(h2, peer_w_q[layer].astype(BF16), sk)
    out = _peer(h2, peer_u[layer].astype(BF16), peer_v[layer].astype(BF16).T, e1, jr, e2, rk2, hs.reshape(n, d))
    return out.reshape(b, s_real, d)
```

```python
import functools
import math

import jax
import jax.numpy as jnp
from jax import lax
from jax.experimental import pallas as pl
from jax.experimental.pallas import tpu as pltpu

F32 = jnp.float32
BF16 = jnp.bfloat16

D_MODEL = 1024
N_META = 16
META_ROWS = 128
CHUNK = 64
ROPE_THETA = 10000.0
EPS = 1e-6
NEG = -1e30
LOG2E = math.log2(math.e)
MAX_FIXED_SHIFT = 60.0

DIFF_HEADS = 4
HEAD_DIM = 64
FOX_HEADS = 8
BRANCH_WIDTH = 512

PEER_HEADS = 8
PEER_KEYS = 128
PEER_HALF = 128
PEER_TOPK = 16

_O_DQ, _O_DK, _O_DV, _O_FQ, _O_FK, _O_FV, _O_FF, _O_GA, _O_GB, _O_END = (
    0, 512, 1024, 1536, 2048, 2560, 3072, 3200, 4224, 5248)

VMEM_LIMIT_BYTES = 56 * 1024 * 1024


def _nt_dot(a, b):
    return lax.dot_general(a, b, (((1,), (1,)), ((), ())), preferred_element_type=F32)


def _largest_divisor(n, cap, mult):
    best = None
    for t in range(mult, min(n, cap) + 1, mult):
        if n % t == 0:
            best = t
    assert best is not None, (n, cap, mult)
    return best


def _group_mean_sq(y, gmat):
    return jnp.dot((y * y).astype(BF16), gmat, preferred_element_type=F32) * (1.0 / HEAD_DIM)


def _rope(y, cos, sin):
    lane = lax.broadcasted_iota(jnp.int32, y.shape, 1)
    first_half = (lane & (HEAD_DIM - 1)) < (HEAD_DIM // 2)
    w = y.shape[1]
    rot = jnp.where(first_half, -pltpu.roll(y, w - HEAD_DIM // 2, 1), pltpu.roll(y, HEAD_DIM // 2, 1))
    return y * cos + rot * sin


def _proj_kernel(x_ref, g_ref, w_ref, cos_ref, sin_ref, gq_ref, gk_ref, fgq_ref, fgk_ref, bf_ref, bg_ref,
                 gmat_ref, dq_ref, dk_ref, dv_ref, fq_ref, fk_ref, fv_ref, lf_ref, ga_ref, gb_ref,
                 *, tm, blocks_per_batch, n_valid):
    x = x_ref[...]
    h = (x * lax.rsqrt(jnp.mean(x * x, axis=-1, keepdims=True) + EPS) * g_ref[...]).astype(BF16)
    gmat = gmat_ref[...]
    cos = jnp.concatenate([cos_ref[...]] * 4, axis=1)
    sin = jnp.concatenate([sin_ref[...]] * 4, axis=1)

    def sec(lo, hi):
        return jnp.dot(h, w_ref[:, lo:hi], preferred_element_type=F32)

    def headnorm(y, gain):
        return y * lax.rsqrt(_group_mean_sq(y, gmat) + EPS) * gain

    scale = HEAD_DIM ** -0.5 * LOG2E
    dq_ref[...] = (_rope(headnorm(sec(_O_DQ, _O_DK), gq_ref[...]), cos, sin) * scale).astype(BF16)
    dk_ref[...] = _rope(headnorm(sec(_O_DK, _O_DV), gk_ref[...]), cos, sin).astype(BF16)
    dv_ref[...] = sec(_O_DV, _O_FQ).astype(BF16)
    fq_ref[...] = (headnorm(sec(_O_FQ, _O_FK), fgq_ref[...]) * scale).astype(BF16)
    fk_ref[...] = headnorm(sec(_O_FK, _O_FV), fgk_ref[...]).astype(BF16)
    fv_ref[...] = sec(_O_FV, _O_FF).astype(BF16)

    z = sec(_O_FF, _O_GA) + bf_ref[...]
    log_f = jnp.minimum(z, 0.0) - jnp.log1p(jnp.exp(-jnp.abs(z)))
    row = (pl.program_id(0) % blocks_per_batch) * tm + lax.broadcasted_iota(jnp.int32, z.shape, 0)
    lf_ref[...] = jnp.where(row < n_valid, log_f, 0.0)

    ga_ref[...] = jax.nn.sigmoid(sec(_O_GA, _O_GB) + bg_ref[:, :D_MODEL]).astype(BF16)
    gb_ref[...] = jax.nn.sigmoid(sec(_O_GB, _O_END) + bg_ref[:, D_MODEL:]).astype(BF16)


def _proj(hin, g, w_all, cos, sin, gq, gk, fgq, fgk, bf, bg, gmat, *, t_int, n_valid):
    n = hin.shape[0]
    tm = _largest_divisor(t_int, 544, 16)
    bpb = t_int // tm
    row = lambda i: (i, 0)
    const = lambda i: (0, 0)
    tab = lambda i: (i % bpb, 0)
    wide = lambda w: pl.BlockSpec((tm, w), row)
    out_shapes = [jax.ShapeDtypeStruct((n, BRANCH_WIDTH), BF16)] * 6 + [
        jax.ShapeDtypeStruct((n, 128), F32),
        jax.ShapeDtypeStruct((n, D_MODEL), BF16), jax.ShapeDtypeStruct((n, D_MODEL), BF16)]
    return pl.pallas_call(
        functools.partial(_proj_kernel, tm=tm, blocks_per_batch=bpb, n_valid=n_valid),
        grid=(n // tm,),
        in_specs=[wide(D_MODEL), pl.BlockSpec((1, D_MODEL), const),
                  pl.BlockSpec((D_MODEL, _O_END), const, pipeline_mode=pl.Buffered(1)),
                  pl.BlockSpec((tm, 128), tab), pl.BlockSpec((tm, 128), tab),
                  pl.BlockSpec((1, BRANCH_WIDTH), const), pl.BlockSpec((1, BRANCH_WIDTH), const),
                  pl.BlockSpec((1, BRANCH_WIDTH), const), pl.BlockSpec((1, BRANCH_WIDTH), const),
                  pl.BlockSpec((1, 128), const), pl.BlockSpec((1, 2 * D_MODEL), const),
                  pl.BlockSpec((BRANCH_WIDTH, BRANCH_WIDTH), const)],
        out_specs=[wide(BRANCH_WIDTH)] * 6 + [wide(128), wide(D_MODEL), wide(D_MODEL)],
        out_shape=out_shapes,
        compiler_params=pltpu.CompilerParams(dimension_semantics=("arbitrary",),
                                             vmem_limit_bytes=VMEM_LIMIT_BYTES),
        name="proj",
    )(hin, g, w_all, cos, sin, gq, gk, fgq, fgk, bf, bg, gmat)


def _cum_kernel(lf_ref, lfm_ref, tri_ref, col_ref, row_ref, rowm_ref, *, n_blocks):
    tri = tri_ref[...]

    def block_cumsum(v, carry):
        v = v * LOG2E
        h1 = v.astype(BF16)
        r1 = v - h1.astype(F32)
        h2 = r1.astype(BF16)
        h3 = (r1 - h2.astype(F32)).astype(BF16)
        return (jnp.dot(tri, h1, preferred_element_type=F32) + jnp.dot(tri, h2, preferred_element_type=F32)
                + jnp.dot(tri, h3, preferred_element_type=F32)) + carry

    c = block_cumsum(lfm_ref[...], jnp.zeros((1, 128), F32))
    rowm_ref[0] = c.T[0:FOX_HEADS, :]
    for blk in range(n_blocks):
        sl = slice(blk * 128, (blk + 1) * 128)
        c = block_cumsum(lf_ref[0, sl, :], c[127:128, :])
        col_ref[0, sl, :] = c
        row_ref[0, :, sl] = c.T[0:FOX_HEADS, :]


def _cum(lf3, lf_meta, tri):
    b, s_real, _ = lf3.shape
    per_b = lambda shape: pl.BlockSpec((1,) + shape, lambda i: (i,) + (0,) * len(shape))
    sds = lambda shape: jax.ShapeDtypeStruct((b,) + shape, F32)
    return pl.pallas_call(
        functools.partial(_cum_kernel, n_blocks=s_real // 128),
        grid=(b,),
        in_specs=[per_b((s_real, 128)), pl.BlockSpec((META_ROWS, 128), lambda i: (0, 0)),
                  pl.BlockSpec((128, 128), lambda i: (0, 0))],
        out_specs=[per_b((s_real, 128)), per_b((FOX_HEADS, s_real)), per_b((FOX_HEADS, META_ROWS))],
        out_shape=[sds((s_real, 128)), sds((FOX_HEADS, s_real)), sds((FOX_HEADS, META_ROWS))],
        compiler_params=pltpu.CompilerParams(dimension_semantics=("arbitrary",)),
        name="cum",
    )(lf3, lf_meta, tri)


def _softmax_step(qg, kblk, vblk, mask, bias, state, shift):
    s = _nt_dot(qg, kblk)
    if bias is not None:
        s = s + bias
    if mask is not None:
        s = jnp.where(mask, s, NEG)
    if shift is not None:
        p = jnp.exp2(s - shift)
        part = p[:, 0:128]
        for j in range(1, p.shape[1] // 128):
            part = part + p[:, j * 128:(j + 1) * 128]
        pv = jnp.dot(p.astype(BF16), vblk, preferred_element_type=F32)
        return (part, pv) if state is None else (state[0] + part, state[1] + pv)
    blk_max = jnp.max(s, axis=1, keepdims=True)
    if state is None:
        p = jnp.exp2(s - blk_max)
        return (blk_max, jnp.sum(p, axis=1, keepdims=True),
                jnp.dot(p.astype(BF16), vblk, preferred_element_type=F32))
    m_old, l, acc = state
    m = jnp.maximum(m_old, blk_max)
    p = jnp.exp2(s - m)
    alpha = jnp.exp2(m_old - m)
    return (m, alpha * l + jnp.sum(p, axis=1, keepdims=True),
            alpha * acc + jnp.dot(p.astype(BF16), vblk, preferred_element_type=F32))


def _split_halves(q_ref, n_groups):
    qs = []
    for g in range(n_groups):
        q = q_ref[0, :, g * 128:(g + 1) * 128]
        lo = lax.broadcasted_iota(jnp.int32, q.shape, 1) < HEAD_DIM
        qs += [jnp.where(lo, q, jnp.zeros_like(q)), jnp.where(lo, jnp.zeros_like(q), q)]
    return qs


def _attention_sweep(q_ref, k_ref, v_ref, km_ref, vm_ref, i, shift, *, tq, n_groups, diag_limit_fn, bias_fn):
    qs = _split_halves(q_ref, n_groups)
    k0 = pl.multiple_of(i * tq, tq)
    r = lax.broadcasted_iota(jnp.int32, (tq, tq + META_ROWS), 0)
    c = lax.broadcasted_iota(jnp.int32, (tq, tq + META_ROWS), 1)
    mask = c < jnp.where(c < tq, diag_limit_fn(r), tq + N_META)
    carry = []
    for g in range(n_groups):
        cols = slice(g * 128, (g + 1) * 128)
        kk = jnp.concatenate([k_ref[0, pl.ds(k0, tq), cols], km_ref[:, cols]], axis=0)
        vv = jnp.concatenate([v_ref[0, pl.ds(k0, tq), cols], vm_ref[:, cols]], axis=0)
        for e in range(2):
            bias = None if bias_fn is None else jnp.concatenate(
                [bias_fn(2 * g + e, k0, tq), bias_fn(2 * g + e, None, META_ROWS)], axis=1)
            carry.append(_softmax_step(qs[2 * g + e], kk, vv, mask, bias, None, shift))

    def body(kc, carry):
        kb = pl.multiple_of(kc * tq, tq)
        out = []
        for g in range(n_groups):
            cols = slice(g * 128, (g + 1) * 128)
            kblk = k_ref[0, pl.ds(kb, tq), cols]
            vblk = v_ref[0, pl.ds(kb, tq), cols]
            for e in range(2):
                bias = None if bias_fn is None else bias_fn(2 * g + e, kb, tq)
                out.append(_softmax_step(qs[2 * g + e], kblk, vblk, None, bias, carry[2 * g + e], shift))
        return tuple(out)

    carry = lax.fori_loop(0, i, body, tuple(carry))
    if shift is None:
        return [acc / l for (_, l, acc) in carry]
    return [acc / jnp.sum(part, axis=1, keepdims=True) for (part, acc) in carry]


def _with_score_bound(bound_ref, run):
    bound = bound_ref[0]
    small = bound <= MAX_FIXED_SHIFT
    pl.when(small)(lambda: run(bound))
    pl.when(jnp.logical_not(small))(lambda: run(None))


def _diff_attn_kernel(bound_ref, q_ref, k_ref, v_ref, km_ref, vm_ref, lam_ref, sg_ref, o_ref, *, tq, lam_init):
    def run(shift):
        outs = _attention_sweep(q_ref, k_ref, v_ref, km_ref, vm_ref, pl.program_id(1), shift, tq=tq,
                                n_groups=DIFF_HEADS, diag_limit_fn=lambda r: (r // CHUNK + 1) * CHUNK, bias_fn=None)
        lam = lam_ref[...]
        lam = (jnp.exp(jnp.sum(lam[0:1] * lam[1:2], axis=1, keepdims=True))
               - jnp.exp(jnp.sum(lam[2:3] * lam[3:4], axis=1, keepdims=True)) + lam_init)
        for h in range(DIFF_HEADS):
            od = outs[2 * h] - lam * outs[2 * h + 1]
            y = od * lax.rsqrt(jnp.mean(od * od, axis=-1, keepdims=True) + EPS) * sg_ref[...]
            o_ref[0, :, h * 128:(h + 1) * 128] = (y * (1.0 - lam_init)).astype(BF16)

    _with_score_bound(bound_ref, run)


def _attention_call(kernel_fn, name, bound, q, k, v, k_meta, v_meta, extra, extra_specs):
    b, s_real, w = q.shape
    tq = min(512, s_real)
    whole = lambda rows: pl.BlockSpec((rows, w), lambda bi, i: (0, 0))
    return pl.pallas_call(
        functools.partial(kernel_fn, tq=tq),
        grid=(b, s_real // tq),
        in_specs=[pl.BlockSpec(memory_space=pltpu.SMEM),
                  pl.BlockSpec((1, tq, w), lambda bi, i: (bi, i, 0)),
                  pl.BlockSpec((1, s_real, w), lambda bi, i: (bi, 0, 0)),
                  pl.BlockSpec((1, s_real, w), lambda bi, i: (bi, 0, 0)),
                  whole(META_ROWS), whole(META_ROWS)] + extra_specs(tq, s_real),
        out_specs=pl.BlockSpec((1, tq, w), lambda bi, i: (bi, i, 0)),
        out_shape=jax.ShapeDtypeStruct((b, s_real, w), BF16),
        compiler_params=pltpu.CompilerParams(dimension_semantics=("arbitrary",) * 2,
                                             vmem_limit_bytes=VMEM_LIMIT_BYTES),
        name=name,
    )(bound, q, k, v, k_meta, v_meta, *extra)


def _score_bound(gq, gk):
    return (1.01 * LOG2E * HEAD_DIM ** 0.5 * jnp.max(jnp.abs(gq)) * jnp.max(jnp.abs(gk))).astype(F32).reshape(1)


def _diff_attention(bound, q, k, v, k_meta, v_meta, lam, sg, *, lam_init):
    specs = lambda tq, s_real: [pl.BlockSpec((4, HEAD_DIM), lambda bi, i: (0, 0)),
                                pl.BlockSpec((1, 128), lambda bi, i: (0, 0))]
    return _attention_call(functools.partial(_diff_attn_kernel, lam_init=lam_init), "diff_attn", bound, q, k, v,
                           k_meta, v_meta, (lam, sg), specs)


def _fox_attn_kernel(bound_ref, q_ref, k_ref, v_ref, km_ref, vm_ref, col_ref, row_ref, rowm_ref, o_ref, *, tq):
    def run(shift):
        cum = col_ref[0]

        def bias_fn(head, k0, width):
            ck = rowm_ref[0, head] if k0 is None else row_ref[0, head, :, pl.ds(k0, width)]
            return cum[:, head:head + 1] - ck

        outs = _attention_sweep(q_ref, k_ref, v_ref, km_ref, vm_ref, pl.program_id(1), shift, tq=tq,
                                n_groups=FOX_HEADS // 2, diag_limit_fn=lambda r: r + 1, bias_fn=bias_fn)
        lo = lax.broadcasted_iota(jnp.int32, (tq, 128), 1) < HEAD_DIM
        for g in range(FOX_HEADS // 2):
            o_ref[0, :, g * 128:(g + 1) * 128] = jnp.where(lo, outs[2 * g], outs[2 * g + 1]).astype(BF16)

    _with_score_bound(bound_ref, run)


def _fox_attention(bound, q, k, v, k_meta, v_meta, cum_col, cum_row, cum_row_meta):
    specs = lambda tq, s_real: [pl.BlockSpec((1, tq, 128), lambda bi, i: (bi, i, 0)),
                                pl.BlockSpec((1, FOX_HEADS, 1, s_real), lambda bi, i: (bi, 0, 0, 0)),
                                pl.BlockSpec((1, FOX_HEADS, 1, META_ROWS), lambda bi, i: (bi, 0, 0, 0))]
    return _attention_call(_fox_attn_kernel, "fox_attn", bound, q, k, v, k_meta, v_meta,
                           (cum_col, cum_row[:, :, None, :], cum_row_meta[:, :, None, :]), specs)


def _merge_kernel(ya_ref, yb_ref, ga_ref, gb_ref, x_ref, wd_ref, wf_ref, wo_ref, g_ref, hs_ref, h2_ref):
    merged = (ga_ref[0].astype(F32) * jnp.dot(ya_ref[0], wd_ref[...], preferred_element_type=F32)
              + gb_ref[0].astype(F32) * jnp.dot(yb_ref[0], wf_ref[...], preferred_element_type=F32))
    hs = x_ref[0] + jnp.dot(merged.astype(BF16), wo_ref[...], preferred_element_type=F32)
    hs_ref[0] = hs
    h2_ref[0] = (hs * lax.rsqrt(jnp.mean(hs * hs, axis=-1, keepdims=True) + EPS) * g_ref[...]).astype(BF16)


def _merge(ya, yb, ga, gb, x, wd, wf, wo, g):
    b, s_real, _ = x.shape
    tm = min(512, s_real)
    blk = lambda w: pl.BlockSpec((1, tm, w), lambda bi, i: (bi, i, 0))
    const = lambda shape: pl.BlockSpec(shape, lambda bi, i: (0, 0))
    return pl.pallas_call(
        _merge_kernel,
        grid=(b, s_real // tm),
        in_specs=[blk(BRANCH_WIDTH), blk(BRANCH_WIDTH), blk(D_MODEL), blk(D_MODEL), blk(D_MODEL),
                  const((BRANCH_WIDTH, D_MODEL)), const((BRANCH_WIDTH, D_MODEL)), const((D_MODEL, D_MODEL)),
                  const((1, D_MODEL))],
        out_specs=[blk(D_MODEL), blk(D_MODEL)],
        out_shape=[jax.ShapeDtypeStruct((b, s_real, D_MODEL), F32), jax.ShapeDtypeStruct((b, s_real, D_MODEL), BF16)],
        compiler_params=pltpu.CompilerParams(dimension_semantics=("arbitrary",) * 2,
                                             vmem_limit_bytes=VMEM_LIMIT_BYTES),
        name="merge",
    )(ya, yb, ga, gb, x, wd, wf, wo, g)


def _route_kernel(h2_ref, wq_ref, sk_ref, e1_ref, jr_ref, e2_ref, rk2_ref, st_ref, val_ref, rank_ref):
    n_lists = 2 * PEER_HEADS
    tn = h2_ref.shape[0]
    q = jnp.dot(h2_ref[...], wq_ref[...], preferred_element_type=F32).astype(BF16)
    for li in range(n_lists):
        st_ref[li] = _nt_dot(sk_ref[li], q[:, li * PEER_HALF:(li + 1) * PEER_HALF])

    def extract(s, index, index_end, on_hit, tie_exact):
        for it in range(PEER_TOPK):
            m = jnp.max(s, axis=0, keepdims=True)
            hit = s == m
            if tie_exact:
                hit = index == jnp.min(jnp.where(hit, index, index_end), axis=0, keepdims=True)
            on_hit(it, m, hit)
            s = jnp.where(hit, -jnp.inf, s)

    def any_tie(removed):
        return jnp.max(jnp.sum(removed, axis=0, keepdims=True)) > float(PEER_TOPK)

    rows = lax.broadcasted_iota(jnp.int32, (PEER_KEYS, tn), 0).astype(F32)

    def list_pair_body(hd, _):
        def run(li, tie_exact):
            state = {"rank": jnp.full((PEER_KEYS, tn), float(PEER_TOPK), F32), "vals": []}

            def on_hit(it, m, hit):
                state["rank"] = jnp.where(hit, float(it), state["rank"])
                state["vals"].append(m)

            extract(st_ref[li], rows, float(PEER_KEYS), on_hit, tie_exact)
            val_ref[li] = jnp.concatenate(state["vals"], axis=0)
            rank_ref[li] = state["rank"]
            return state["rank"]

        ranks = [run(2 * hd + half_id, False) for half_id in range(2)]
        for half_id in range(2):
            @pl.when(any_tie(jnp.where(ranks[half_id] < float(PEER_TOPK), 1.0, 0.0)))
            def _():
                run(2 * hd + half_id, True)

        return 0

    lax.fori_loop(0, PEER_HEADS, list_pair_body, 0)

    n_cand = PEER_TOPK * PEER_TOPK
    half = PEER_TOPK // 2
    iota8 = lax.broadcasted_iota(jnp.int32, (half, tn), 0)
    pos = jnp.concatenate(
        [lax.broadcasted_iota(jnp.int32, (PEER_TOPK, tn), 0)]
        + [iota8 + PEER_TOPK * r for r in range(1, half)]
        + [(iota8 + half) * PEER_TOPK], axis=0).astype(F32)

    def head_body(hd, _):
        a = val_ref[2 * hd]
        b = val_ref[2 * hd + 1]
        cand = jnp.concatenate(
            [a[0:1, :] + b] + [a[r:r + 1, :] + b[0:half, :] for r in range(1, half)]
            + [a[half:PEER_TOPK, :] + b[0:1, :]], axis=0)
        m0 = a[0:1, :] + b[0:1, :]

        def run(tie_exact):
            state = {"sel": jnp.zeros(cand.shape, F32), "z": jnp.zeros((1, tn), F32)}

            def on_hit(it, m, hit):
                state["sel"] = jnp.where(hit, 1.0, state["sel"])
                state["z"] = state["z"] + jnp.exp(m - m0)

            extract(cand, pos, float(n_cand), on_hit, tie_exact)
            sel = state["sel"]
            rank1 = rank_ref[2 * hd]
            jr = jnp.zeros(rank1.shape, F32)
            for r in range(PEER_TOPK):
                if r == 0:
                    j_r = jnp.sum(sel[0:PEER_TOPK, :], axis=0, keepdims=True)
                elif r < half:
                    j_r = jnp.sum(sel[PEER_TOPK + half * (r - 1):PEER_TOPK + half * r, :], axis=0, keepdims=True)
                else:
                    lo = PEER_TOPK + half * (half - 1) + (r - half)
                    j_r = sel[lo:lo + 1, :]
                jr = jnp.where(rank1 == float(r), j_r, jr)
            jr_ref[hd] = jr
            e1_ref[hd] = jnp.exp(st_ref[2 * hd] - a[0:1, :]) / state["z"]
            return sel

        sel = run(False)

        @pl.when(any_tie(sel))
        def _():
            run(True)

        e2_ref[hd] = jnp.exp(st_ref[2 * hd + 1] - b[0:1, :]).astype(BF16)
        rk2_ref[hd] = rank_ref[2 * hd + 1].astype(BF16)
        return 0

    lax.fori_loop(0, PEER_HEADS, head_body, 0)


def _route(h2, wq, sk):
    n = h2.shape[0]
    tn = _largest_divisor(n, 512, 128)
    tok = lambda i: (0, 0, i)
    tab = lambda dt: jax.ShapeDtypeStruct((PEER_HEADS, PEER_KEYS, n), dt)
    return pl.pallas_call(
        _route_kernel,
        grid=(n // tn,),
        in_specs=[pl.BlockSpec((tn, D_MODEL), lambda i: (i, 0)),
                  pl.BlockSpec((D_MODEL, 2 * PEER_HEADS * PEER_HALF), lambda i: (0, 0)),
                  pl.BlockSpec((2 * PEER_HEADS, PEER_KEYS, PEER_HALF), lambda i: (0, 0, 0))],
        out_specs=[pl.BlockSpec((PEER_HEADS, PEER_KEYS, tn), tok)] * 4,
        out_shape=[tab(F32), tab(F32), tab(BF16), tab(BF16)],
        scratch_shapes=[pltpu.VMEM((2 * PEER_HEADS, PEER_KEYS, tn), F32),
                        pltpu.VMEM((2 * PEER_HEADS, PEER_TOPK, tn), F32),
                        pltpu.VMEM((2 * PEER_HEADS, PEER_KEYS, tn), F32)],
        compiler_params=pltpu.CompilerParams(dimension_semantics=("arbitrary",),
                                             vmem_limit_bytes=VMEM_LIMIT_BYTES),
        name="route",
    )(h2, wq, sk)


def _peer_kernel(h2_ref, u_ref, vt_ref, e1_ref, jr_ref, e2_ref, rk2_ref, hs_ref, o_ref, acc_ref, *, rows_per_step):
    c = pl.program_id(1)
    tn = h2_ref.shape[0]

    @pl.when(c == 0)
    def _():
        acc_ref[...] = jnp.zeros_like(acc_ref)

    s_t = _nt_dot(u_ref[...], h2_ref[...])
    act = (s_t * (1.0 + lax.erf(s_t * (2.0 ** -0.5)))).astype(BF16)
    shape = (PEER_KEYS, tn)
    rows16 = lambda row: jnp.concatenate([jnp.broadcast_to(row, (16, tn)).astype(BF16)] * (PEER_KEYS // 16), axis=0)
    ws = []
    for rr in range(rows_per_step):
        gate = None
        for hd in range(PEER_HEADS):
            term = jnp.where(rk2_ref[hd] < rows16(jr_ref[hd, rr:rr + 1, :]),
                             e2_ref[hd] * rows16(e1_ref[hd, rr:rr + 1, :]), jnp.zeros(shape, BF16))
            gate = term if gate is None else gate + term
        ws.append(act[rr * PEER_KEYS:(rr + 1) * PEER_KEYS, :] * gate)
    w_t = jnp.concatenate(ws, axis=0)
    acc_ref[...] += jnp.dot(vt_ref[...], w_t, preferred_element_type=F32)

    @pl.when(c == pl.num_programs(1) - 1)
    def _():
        o_ref[...] = hs_ref[...] + acc_ref[...].T


def _peer(h2, u, vt, e1, jr, e2, rk2, hs):
    n = h2.shape[0]
    tn = _largest_divisor(n, 1024, 128)
    rows_per_step = 8
    ec = rows_per_step * PEER_KEYS
    n_chunks = PEER_KEYS // rows_per_step
    tab = pl.BlockSpec((PEER_HEADS, PEER_KEYS, tn), lambda i, c: (0, 0, i))
    row_tab = pl.BlockSpec((PEER_HEADS, rows_per_step, tn), lambda i, c: (0, c, i))
    return pl.pallas_call(
        functools.partial(_peer_kernel, rows_per_step=rows_per_step),
        grid=(n // tn, n_chunks),
        in_specs=[pl.BlockSpec((tn, D_MODEL), lambda i, c: (i, 0)),
                  pl.BlockSpec((ec, D_MODEL), lambda i, c: (c, 0)),
                  pl.BlockSpec((D_MODEL, ec), lambda i, c: (0, c)),
                  row_tab, row_tab, tab, tab,
                  pl.BlockSpec((tn, D_MODEL), lambda i, c: (i, 0))],
        out_specs=pl.BlockSpec((tn, D_MODEL), lambda i, c: (i, 0)),
        out_shape=jax.ShapeDtypeStruct((n, D_MODEL), F32),
        scratch_shapes=[pltpu.VMEM((D_MODEL, tn), F32)],
        compiler_params=pltpu.CompilerParams(dimension_semantics=("arbitrary", "arbitrary"),
                                             vmem_limit_bytes=VMEM_LIMIT_BYTES),
        name="peer",
    )(h2, u, vt, e1, jr, e2, rk2, hs)


def _relayout_w_in(w):
    hq = lambda lo: (w[:, lo:lo + 512].reshape(D_MODEL, 2, DIFF_HEADS, HEAD_DIM)
                     .transpose(0, 2, 1, 3).reshape(D_MODEL, 512))
    ff = jnp.pad(w[:, 3072:3080], ((0, 0), (0, 120)))
    return jnp.concatenate([hq(0), hq(512), w[:, 1024:3072], ff, w[:, 3080:5128]], axis=1).astype(BF16)


def kernel(x, meta_tokens, norm_mix_g, w_in, b_gate, b_forget, diff_qnorm_g, diff_knorm_g, diff_lambda,
           diff_subln_g, fox_qnorm_g, fox_knorm_g, w_branch_diff, w_branch_fox, w_out, norm_ffn_g, peer_w_q,
           peer_subkeys, peer_u, peer_v):
    b, s_real, d = x.shape
    assert d == D_MODEL and s_real % 128 == 0 and norm_mix_g.shape[0] == 1
    layer = 0
    lam_init = 0.8 - 0.6 * math.exp(-0.3 * layer)

    meta_blk = jnp.pad(meta_tokens.astype(F32), ((0, META_ROWS - N_META), (0, 0)))

    def rope_tables(pos):
        inv_freq = ROPE_THETA ** (-jnp.arange(0, HEAD_DIM, 2, dtype=F32) / HEAD_DIM)
        ang = pos.astype(F32)[:, None] * inv_freq[None, :]
        return jnp.tile(jnp.cos(ang), (1, 4)), jnp.tile(jnp.sin(ang), (1, 4))

    tile8 = lambda g: jnp.tile(g.astype(F32), 8)[None, :]
    gidx = jnp.arange(BRANCH_WIDTH) // HEAD_DIM
    gmat = (gidx[:, None] == gidx[None, :]).astype(BF16)
    bf = jnp.pad(b_forget[layer].astype(F32), (0, 120))[None, :]
    w_all = _relayout_w_in(w_in[layer])

    def project(rows, pos, n_valid):
        cos, sin = rope_tables(pos)
        return _proj(rows, norm_mix_g[layer][None, :], w_all, cos, sin, tile8(diff_qnorm_g[layer]),
                     tile8(diff_knorm_g[layer]), tile8(fox_qnorm_g[layer]), tile8(fox_knorm_g[layer]),
                     bf, b_gate[layer][None, :], gmat, t_int=pos.shape[0], n_valid=n_valid)

    dq, dk, dv, fq, fk, fv, lf, ga, gb = project(x.reshape(b * s_real, d), N_META + jnp.arange(s_real), s_real)
    _, dk_m, dv_m, _, fk_m, fv_m, lf_m, _, _ = project(meta_blk, jnp.arange(META_ROWS), N_META)

    r3 = lambda a: a.reshape(b, s_real, a.shape[-1])
    tri = (jnp.arange(128)[:, None] >= jnp.arange(128)[None, :]).astype(BF16)
    cum_col, cum_row, cum_row_meta = _cum(r3(lf), lf_m, tri)

    ya = _diff_attention(_score_bound(diff_qnorm_g[layer], diff_knorm_g[layer]), r3(dq), r3(dk), r3(dv), dk_m, dv_m,
                         diff_lambda[layer].astype(F32), diff_subln_g[layer].astype(F32)[None, :], lam_init=lam_init)
    yb = _fox_attention(_score_bound(fox_qnorm_g[layer], fox_knorm_g[layer]), r3(fq), r3(fk), r3(fv), fk_m, fv_m,
                        cum_col, cum_row, cum_row_meta)

    hs, h2 = _merge(ya, yb, r3(ga), r3(gb), x, w_branch_diff[layer].astype(BF16), w_branch_fox[layer].astype(BF16),
                    w_out[layer].astype(BF16), norm_ffn_g[layer][None, :])

    n = b * s_real
    h2 = h2.reshape(n, d)
    sk = peer_subkeys[layer].astype(BF16).reshape(2 * PEER_HEADS, PEER_KEYS, PEER_HALF)
    e1, jr, e2, rk2 = _route(h2, peer_w_q[layer].astype(BF16), sk)
    half_vt = (0.5 * peer_v[layer]).astype(BF16).T
    out = _peer(h2, peer_u[layer].astype(BF16), half_vt, e1, jr, e2, rk2, hs.reshape(n, d))
    return out.reshape(b, s_real, d)
```

```python
import functools
import math

import jax
import jax.numpy as jnp
from jax import lax
from jax.experimental import pallas as pl
from jax.experimental.pallas import tpu as pltpu

F32 = jnp.float32
BF16 = jnp.bfloat16

D_MODEL = 1024
N_META = 16
META_ROWS = 128
CHUNK = 64
ROPE_THETA = 10000.0
EPS = 1e-6
NEG = -1e30
LOG2E = math.log2(math.e)
MAX_FIXED_SHIFT = 60.0

DIFF_HEADS = 4
HEAD_DIM = 64
FOX_HEADS = 8
BRANCH_WIDTH = 512

PEER_HEADS = 8
PEER_KEYS = 128
PEER_HALF = 128
PEER_TOPK = 16

_O_DQ, _O_DK, _O_DV, _O_FQ, _O_FK, _O_FV, _O_FF, _O_GA, _O_GB, _O_END = (
    0, 512, 1024, 1536, 2048, 2560, 3072, 3200, 4224, 5248)

VMEM_LIMIT_BYTES = 56 * 1024 * 1024


def _nt_dot(a, b):
    return lax.dot_general(a, b, (((1,), (1,)), ((), ())), preferred_element_type=F32)


def _largest_divisor(n, cap, mult):
    best = None
    for t in range(mult, min(n, cap) + 1, mult):
        if n % t == 0:
            best = t
    assert best is not None, (n, cap, mult)
    return best


def _group_mean_sq(y, gmat):
    return jnp.dot((y * y).astype(BF16), gmat, preferred_element_type=F32) * (1.0 / HEAD_DIM)


def _rope(y, cos, sin):
    lane = lax.broadcasted_iota(jnp.int32, y.shape, 1)
    first_half = (lane & (HEAD_DIM - 1)) < (HEAD_DIM // 2)
    w = y.shape[1]
    rot = jnp.where(first_half, -pltpu.roll(y, w - HEAD_DIM // 2, 1), pltpu.roll(y, HEAD_DIM // 2, 1))
    return y * cos + rot * sin


def _proj_kernel(x_ref, g_ref, w_ref, cos_ref, sin_ref, gq_ref, gk_ref, fgq_ref, fgk_ref, bf_ref, bg_ref,
                 gmat_ref, dq_ref, dk_ref, dv_ref, fq_ref, fk_ref, fv_ref, lf_ref, ga_ref, gb_ref,
                 *, tm, blocks_per_batch, n_valid):
    x = x_ref[...]
    h = (x * lax.rsqrt(jnp.mean(x * x, axis=-1, keepdims=True) + EPS) * g_ref[...]).astype(BF16)
    gmat = gmat_ref[...]
    cos = jnp.concatenate([cos_ref[...]] * 4, axis=1)
    sin = jnp.concatenate([sin_ref[...]] * 4, axis=1)

    def sec(lo, hi):
        return jnp.dot(h, w_ref[:, lo:hi], preferred_element_type=F32)

    def headnorm(y, gain):
        return y * lax.rsqrt(_group_mean_sq(y, gmat) + EPS) * gain

    scale = HEAD_DIM ** -0.5 * LOG2E
    dq_ref[...] = (_rope(headnorm(sec(_O_DQ, _O_DK), gq_ref[...]), cos, sin) * scale).astype(BF16)
    dk_ref[...] = _rope(headnorm(sec(_O_DK, _O_DV), gk_ref[...]), cos, sin).astype(BF16)
    dv_ref[...] = sec(_O_DV, _O_FQ).astype(BF16)
    fq_ref[...] = (headnorm(sec(_O_FQ, _O_FK), fgq_ref[...]) * scale).astype(BF16)
    fk_ref[...] = headnorm(sec(_O_FK, _O_FV), fgk_ref[...]).astype(BF16)
    fv_ref[...] = sec(_O_FV, _O_FF).astype(BF16)

    z = sec(_O_FF, _O_GA) + bf_ref[...]
    log_f = jnp.minimum(z, 0.0) - jnp.log1p(jnp.exp(-jnp.abs(z)))
    row = (pl.program_id(0) % blocks_per_batch) * tm + lax.broadcasted_iota(jnp.int32, z.shape, 0)
    lf_ref[...] = jnp.where(row < n_valid, log_f, 0.0)

    ga_ref[...] = jax.nn.sigmoid(sec(_O_GA, _O_GB) + bg_ref[:, :D_MODEL]).astype(BF16)
    gb_ref[...] = jax.nn.sigmoid(sec(_O_GB, _O_END) + bg_ref[:, D_MODEL:]).astype(BF16)


def _proj(hin, g, w_all, cos, sin, gq, gk, fgq, fgk, bf, bg, gmat, *, t_int, n_valid):
    n = hin.shape[0]
    tm = _largest_divisor(t_int, 544, 16)
    bpb = t_int // tm
    row = lambda i: (i, 0)
    const = lambda i: (0, 0)
    tab = lambda i: (i % bpb, 0)
    wide = lambda w: pl.BlockSpec((tm, w), row)
    out_shapes = [jax.ShapeDtypeStruct((n, BRANCH_WIDTH), BF16)] * 6 + [
        jax.ShapeDtypeStruct((n, 128), F32),
        jax.ShapeDtypeStruct((n, D_MODEL), BF16), jax.ShapeDtypeStruct((n, D_MODEL), BF16)]
    return pl.pallas_call(
        functools.partial(_proj_kernel, tm=tm, blocks_per_batch=bpb, n_valid=n_valid),
        grid=(n // tm,),
        in_specs=[wide(D_MODEL), pl.BlockSpec((1, D_MODEL), const),
                  pl.BlockSpec((D_MODEL, _O_END), const, pipeline_mode=pl.Buffered(1)),
                  pl.BlockSpec((tm, 128), tab), pl.BlockSpec((tm, 128), tab),
                  pl.BlockSpec((1, BRANCH_WIDTH), const), pl.BlockSpec((1, BRANCH_WIDTH), const),
                  pl.BlockSpec((1, BRANCH_WIDTH), const), pl.BlockSpec((1, BRANCH_WIDTH), const),
                  pl.BlockSpec((1, 128), const), pl.BlockSpec((1, 2 * D_MODEL), const),
                  pl.BlockSpec((BRANCH_WIDTH, BRANCH_WIDTH), const)],
        out_specs=[wide(BRANCH_WIDTH)] * 6 + [wide(128), wide(D_MODEL), wide(D_MODEL)],
        out_shape=out_shapes,
        compiler_params=pltpu.CompilerParams(dimension_semantics=("arbitrary",),
                                             vmem_limit_bytes=VMEM_LIMIT_BYTES),
        name="proj",
    )(hin, g, w_all, cos, sin, gq, gk, fgq, fgk, bf, bg, gmat)


def _cum_kernel(lf_ref, lfm_ref, tri_ref, col_ref, row_ref, rowm_ref, *, n_blocks):
    tri = tri_ref[...]

    def block_cumsum(v, carry):
        v = v * LOG2E
        h1 = v.astype(BF16)
        r1 = v - h1.astype(F32)
        h2 = r1.astype(BF16)
        h3 = (r1 - h2.astype(F32)).astype(BF16)
        return (jnp.dot(tri, h1, preferred_element_type=F32) + jnp.dot(tri, h2, preferred_element_type=F32)
                + jnp.dot(tri, h3, preferred_element_type=F32)) + carry

    c = block_cumsum(lfm_ref[...], jnp.zeros((1, 128), F32))
    rowm_ref[0] = c.T[0:FOX_HEADS, :]
    for blk in range(n_blocks):
        sl = slice(blk * 128, (blk + 1) * 128)
        c = block_cumsum(lf_ref[0, sl, :], c[127:128, :])
        col_ref[0, sl, :] = c
        row_ref[0, :, sl] = c.T[0:FOX_HEADS, :]


def _cum(lf3, lf_meta, tri):
    b, s_real, _ = lf3.shape
    per_b = lambda shape: pl.BlockSpec((1,) + shape, lambda i: (i,) + (0,) * len(shape))
    sds = lambda shape: jax.ShapeDtypeStruct((b,) + shape, F32)
    return pl.pallas_call(
        functools.partial(_cum_kernel, n_blocks=s_real // 128),
        grid=(b,),
        in_specs=[per_b((s_real, 128)), pl.BlockSpec((META_ROWS, 128), lambda i: (0, 0)),
                  pl.BlockSpec((128, 128), lambda i: (0, 0))],
        out_specs=[per_b((s_real, 128)), per_b((FOX_HEADS, s_real)), per_b((FOX_HEADS, META_ROWS))],
        out_shape=[sds((s_real, 128)), sds((FOX_HEADS, s_real)), sds((FOX_HEADS, META_ROWS))],
        compiler_params=pltpu.CompilerParams(dimension_semantics=("arbitrary",)),
        name="cum",
    )(lf3, lf_meta, tri)


def _softmax_step(qg, kblk, vblk, mask, bias, state, shift):
    s = _nt_dot(qg, kblk)
    if bias is not None:
        s = s + bias
    if mask is not None:
        s = jnp.where(mask, s, NEG)
    if shift is not None:
        p = jnp.exp2(s - shift)
        part = p[:, 0:128]
        for j in range(1, p.shape[1] // 128):
            part = part + p[:, j * 128:(j + 1) * 128]
        pv = jnp.dot(p.astype(BF16), vblk, preferred_element_type=F32)
        return (part, pv) if state is None else (state[0] + part, state[1] + pv)
    blk_max = jnp.max(s, axis=1, keepdims=True)
    if state is None:
        p = jnp.exp2(s - blk_max)
        return (blk_max, jnp.sum(p, axis=1, keepdims=True),
                jnp.dot(p.astype(BF16), vblk, preferred_element_type=F32))
    m_old, l, acc = state
    m = jnp.maximum(m_old, blk_max)
    p = jnp.exp2(s - m)
    alpha = jnp.exp2(m_old - m)
    return (m, alpha * l + jnp.sum(p, axis=1, keepdims=True),
            alpha * acc + jnp.dot(p.astype(BF16), vblk, preferred_element_type=F32))


def _split_halves(q_ref, n_groups):
    qs = []
    for g in range(n_groups):
        q = q_ref[0, :, g * 128:(g + 1) * 128]
        lo = lax.broadcasted_iota(jnp.int32, q.shape, 1) < HEAD_DIM
        qs += [jnp.where(lo, q, jnp.zeros_like(q)), jnp.where(lo, jnp.zeros_like(q), q)]
    return qs


def _attention_sweep(q_ref, k_ref, v_ref, km_ref, vm_ref, i, shift, *, tq, n_groups, diag_limit_fn, bias_fn):
    qs = _split_halves(q_ref, n_groups)
    k0 = pl.multiple_of(i * tq, tq)
    r = lax.broadcasted_iota(jnp.int32, (tq, tq + META_ROWS), 0)
    c = lax.broadcasted_iota(jnp.int32, (tq, tq + META_ROWS), 1)
    mask = c < jnp.where(c < tq, diag_limit_fn(r), tq + N_META)
    carry = []
    for g in range(n_groups):
        cols = slice(g * 128, (g + 1) * 128)
        kk = jnp.concatenate([k_ref[0, pl.ds(k0, tq), cols], km_ref[:, cols]], axis=0)
        vv = jnp.concatenate([v_ref[0, pl.ds(k0, tq), cols], vm_ref[:, cols]], axis=0)
        for e in range(2):
            bias = None if bias_fn is None else jnp.concatenate(
                [bias_fn(2 * g + e, k0, tq), bias_fn(2 * g + e, None, META_ROWS)], axis=1)
            carry.append(_softmax_step(qs[2 * g + e], kk, vv, mask, bias, None, shift))

    def body(kc, carry):
        kb = pl.multiple_of(kc * tq, tq)
        out = []
        for g in range(n_groups):
            cols = slice(g * 128, (g + 1) * 128)
            kblk = k_ref[0, pl.ds(kb, tq), cols]
            vblk = v_ref[0, pl.ds(kb, tq), cols]
            for e in range(2):
                bias = None if bias_fn is None else bias_fn(2 * g + e, kb, tq)
                out.append(_softmax_step(qs[2 * g + e], kblk, vblk, None, bias, carry[2 * g + e], shift))
        return tuple(out)

    carry = lax.fori_loop(0, i, body, tuple(carry))
    if shift is None:
        return [acc / l for (_, l, acc) in carry]
    return [acc / jnp.sum(part, axis=1, keepdims=True) for (part, acc) in carry]


def _with_score_bound(bound_ref, run):
    bound = bound_ref[0]
    small = bound <= MAX_FIXED_SHIFT
    pl.when(small)(lambda: run(bound))
    pl.when(jnp.logical_not(small))(lambda: run(None))


def _diff_attn_kernel(bound_ref, q_ref, k_ref, v_ref, km_ref, vm_ref, lam_ref, sg_ref, o_ref, *, tq, lam_init):
    def run(shift):
        outs = _attention_sweep(q_ref, k_ref, v_ref, km_ref, vm_ref, pl.program_id(1), shift, tq=tq,
                                n_groups=DIFF_HEADS, diag_limit_fn=lambda r: (r // CHUNK + 1) * CHUNK, bias_fn=None)
        lam = lam_ref[...]
        lam = (jnp.exp(jnp.sum(lam[0:1] * lam[1:2], axis=1, keepdims=True))
               - jnp.exp(jnp.sum(lam[2:3] * lam[3:4], axis=1, keepdims=True)) + lam_init)
        for h in range(DIFF_HEADS):
            od = outs[2 * h] - lam * outs[2 * h + 1]
            y = od * lax.rsqrt(jnp.mean(od * od, axis=-1, keepdims=True) + EPS) * sg_ref[...]
            o_ref[0, :, h * 128:(h + 1) * 128] = (y * (1.0 - lam_init)).astype(BF16)

    _with_score_bound(bound_ref, run)


def _attention_call(kernel_fn, name, bound, q, k, v, k_meta, v_meta, extra, extra_specs):
    b, s_real, w = q.shape
    tq = min(512, s_real)
    whole = lambda rows: pl.BlockSpec((rows, w), lambda bi, i: (0, 0))
    return pl.pallas_call(
        functools.partial(kernel_fn, tq=tq),
        grid=(b, s_real // tq),
        in_specs=[pl.BlockSpec(memory_space=pltpu.SMEM),
                  pl.BlockSpec((1, tq, w), lambda bi, i: (bi, i, 0)),
                  pl.BlockSpec((1, s_real, w), lambda bi, i: (bi, 0, 0)),
                  pl.BlockSpec((1, s_real, w), lambda bi, i: (bi, 0, 0)),
                  whole(META_ROWS), whole(META_ROWS)] + extra_specs(tq, s_real),
        out_specs=pl.BlockSpec((1, tq, w), lambda bi, i: (bi, i, 0)),
        out_shape=jax.ShapeDtypeStruct((b, s_real, w), BF16),
        compiler_params=pltpu.CompilerParams(dimension_semantics=("arbitrary",) * 2,
                                             vmem_limit_bytes=VMEM_LIMIT_BYTES),
        name=name,
    )(bound, q, k, v, k_meta, v_meta, *extra)


def _score_bound(gq, gk):
    return (1.01 * LOG2E * HEAD_DIM ** 0.5 * jnp.max(jnp.abs(gq)) * jnp.max(jnp.abs(gk))).astype(F32).reshape(1)


def _diff_attention(bound, q, k, v, k_meta, v_meta, lam, sg, *, lam_init):
    specs = lambda tq, s_real: [pl.BlockSpec((4, HEAD_DIM), lambda bi, i: (0, 0)),
                                pl.BlockSpec((1, 128), lambda bi, i: (0, 0))]
    return _attention_call(functools.partial(_diff_attn_kernel, lam_init=lam_init), "diff_attn", bound, q, k, v,
                           k_meta, v_meta, (lam, sg), specs)


def _fox_attn_kernel(bound_ref, q_ref, k_ref, v_ref, km_ref, vm_ref, col_ref, row_ref, rowm_ref, o_ref, *, tq):
    def run(shift):
        cum = col_ref[0]

        def bias_fn(head, k0, width):
            ck = rowm_ref[0, head] if k0 is None else row_ref[0, head, :, pl.ds(k0, width)]
            return cum[:, head:head + 1] - ck

        outs = _attention_sweep(q_ref, k_ref, v_ref, km_ref, vm_ref, pl.program_id(1), shift, tq=tq,
                                n_groups=FOX_HEADS // 2, diag_limit_fn=lambda r: r + 1, bias_fn=bias_fn)
        lo = lax.broadcasted_iota(jnp.int32, (tq, 128), 1) < HEAD_DIM
        for g in range(FOX_HEADS // 2):
            o_ref[0, :, g * 128:(g + 1) * 128] = jnp.where(lo, outs[2 * g], outs[2 * g + 1]).astype(BF16)

    _with_score_bound(bound_ref, run)


def _fox_attention(bound, q, k, v, k_meta, v_meta, cum_col, cum_row, cum_row_meta):
    specs = lambda tq, s_real: [pl.BlockSpec((1, tq, 128), lambda bi, i: (bi, i, 0)),
                                pl.BlockSpec((1, FOX_HEADS, 1, s_real), lambda bi, i: (bi, 0, 0, 0)),
                                pl.BlockSpec((1, FOX_HEADS, 1, META_ROWS), lambda bi, i: (bi, 0, 0, 0))]
    return _attention_call(_fox_attn_kernel, "fox_attn", bound, q, k, v, k_meta, v_meta,
                           (cum_col, cum_row[:, :, None, :], cum_row_meta[:, :, None, :]), specs)


def _merge_kernel(ya_ref, yb_ref, ga_ref, gb_ref, x_ref, wd_ref, wf_ref, wo_ref, g_ref, hs_ref, h2_ref):
    merged = (ga_ref[0].astype(F32) * jnp.dot(ya_ref[0], wd_ref[...], preferred_element_type=F32)
              + gb_ref[0].astype(F32) * jnp.dot(yb_ref[0], wf_ref[...], preferred_element_type=F32))
    hs = x_ref[0] + jnp.dot(merged.astype(BF16), wo_ref[...], preferred_element_type=F32)
    hs_ref[0] = hs
    h2_ref[0] = (hs * lax.rsqrt(jnp.mean(hs * hs, axis=-1, keepdims=True) + EPS) * g_ref[...]).astype(BF16)


def _merge(ya, yb, ga, gb, x, wd, wf, wo, g):
    b, s_real, _ = x.shape
    tm = min(512, s_real)
    blk = lambda w: pl.BlockSpec((1, tm, w), lambda bi, i: (bi, i, 0))
    const = lambda shape: pl.BlockSpec(shape, lambda bi, i: (0, 0))
    return pl.pallas_call(
        _merge_kernel,
        grid=(b, s_real // tm),
        in_specs=[blk(BRANCH_WIDTH), blk(BRANCH_WIDTH), blk(D_MODEL), blk(D_MODEL), blk(D_MODEL),
                  const((BRANCH_WIDTH, D_MODEL)), const((BRANCH_WIDTH, D_MODEL)), const((D_MODEL, D_MODEL)),
                  const((1, D_MODEL))],
        out_specs=[blk(D_MODEL), blk(D_MODEL)],
        out_shape=[jax.ShapeDtypeStruct((b, s_real, D_MODEL), F32), jax.ShapeDtypeStruct((b, s_real, D_MODEL), BF16)],
        compiler_params=pltpu.CompilerParams(dimension_semantics=("arbitrary",) * 2,
                                             vmem_limit_bytes=VMEM_LIMIT_BYTES),
        name="merge",
    )(ya, yb, ga, gb, x, wd, wf, wo, g)


def _route_kernel(h2_ref, wq_ref, sk_ref, e1_ref, jr_ref, e2_ref, rk2_ref, st_ref, val_ref, rank_ref):
    n_lists = 2 * PEER_HEADS
    tn = h2_ref.shape[0]
    q = jnp.dot(h2_ref[...], wq_ref[...], preferred_element_type=F32).astype(BF16)
    for li in range(n_lists):
        st_ref[li] = _nt_dot(sk_ref[li], q[:, li * PEER_HALF:(li + 1) * PEER_HALF])

    def extract(s, index, index_end, on_hit, tie_exact):
        for it in range(PEER_TOPK):
            m = jnp.max(s, axis=0, keepdims=True)
            hit = s == m
            if tie_exact:
                hit = index == jnp.min(jnp.where(hit, index, index_end), axis=0, keepdims=True)
            on_hit(it, m, hit)
            s = jnp.where(hit, -jnp.inf, s)

    def any_tie(removed):
        return jnp.max(jnp.sum(removed, axis=0, keepdims=True)) > float(PEER_TOPK)

    rows = lax.broadcasted_iota(jnp.int32, (PEER_KEYS, tn), 0).astype(F32)

    def list_pair_body(hd, _):
        def run(li, tie_exact):
            state = {"rank": jnp.full((PEER_KEYS, tn), float(PEER_TOPK), F32), "vals": []}

            def on_hit(it, m, hit):
                state["rank"] = jnp.where(hit, float(it), state["rank"])
                state["vals"].append(m)

            extract(st_ref[li], rows, float(PEER_KEYS), on_hit, tie_exact)
            val_ref[li] = jnp.concatenate(state["vals"], axis=0)
            rank_ref[li] = state["rank"]
            return state["rank"]

        ranks = [run(2 * hd + half_id, False) for half_id in range(2)]
        for half_id in range(2):
            @pl.when(any_tie(jnp.where(ranks[half_id] < float(PEER_TOPK), 1.0, 0.0)))
            def _():
                run(2 * hd + half_id, True)

        return 0

    lax.fori_loop(0, PEER_HEADS, list_pair_body, 0)

    n_cand = PEER_TOPK * PEER_TOPK
    half = PEER_TOPK // 2
    iota8 = lax.broadcasted_iota(jnp.int32, (half, tn), 0)
    pos = jnp.concatenate(
        [lax.broadcasted_iota(jnp.int32, (PEER_TOPK, tn), 0)]
        + [iota8 + PEER_TOPK * r for r in range(1, half)]
        + [(iota8 + half) * PEER_TOPK], axis=0).astype(F32)

    def head_body(hd, _):
        a = val_ref[2 * hd]
        b = val_ref[2 * hd + 1]
        cand = jnp.concatenate(
            [a[0:1, :] + b] + [a[r:r + 1, :] + b[0:half, :] for r in range(1, half)]
            + [a[half:PEER_TOPK, :] + b[0:1, :]], axis=0)
        m0 = a[0:1, :] + b[0:1, :]

        def run(tie_exact):
            state = {"sel": jnp.zeros(cand.shape, F32), "z": jnp.zeros((1, tn), F32)}

            def on_hit(it, m, hit):
                state["sel"] = jnp.where(hit, 1.0, state["sel"])
                state["z"] = state["z"] + jnp.exp(m - m0)

            extract(cand, pos, float(n_cand), on_hit, tie_exact)
            sel = state["sel"]
            rank1 = rank_ref[2 * hd]
            jr = jnp.zeros(rank1.shape, F32)
            for r in range(PEER_TOPK):
                if r == 0:
                    j_r = jnp.sum(sel[0:PEER_TOPK, :], axis=0, keepdims=True)
                elif r < half:
                    j_r = jnp.sum(sel[PEER_TOPK + half * (r - 1):PEER_TOPK + half * r, :], axis=0, keepdims=True)
                else:
                    lo = PEER_TOPK + half * (half - 1) + (r - half)
                    j_r = sel[lo:lo + 1, :]
                jr = jnp.where(rank1 == float(r), j_r, jr)
            jr_ref[hd] = jr
            e1_ref[hd] = jnp.exp(st_ref[2 * hd] - a[0:1, :]) / state["z"]
            return sel

        sel = run(False)

        @pl.when(any_tie(sel))
        def _():
            run(True)

        e2_ref[hd] = jnp.exp(st_ref[2 * hd + 1] - b[0:1, :]).astype(BF16)
        rk2_ref[hd] = rank_ref[2 * hd + 1].astype(BF16)
        return 0

    lax.fori_loop(0, PEER_HEADS, head_body, 0)


def _route(h2, wq, sk):
    n = h2.shape[0]
    tn = _largest_divisor(n, 512, 128)
    tok = lambda i: (0, 0, i)
    tab = lambda dt: jax.ShapeDtypeStruct((PEER_HEADS, PEER_KEYS, n), dt)
    return pl.pallas_call(
        _route_kernel,
        grid=(n // tn,),
        in_specs=[pl.BlockSpec((tn, D_MODEL), lambda i: (i, 0)),
                  pl.BlockSpec((D_MODEL, 2 * PEER_HEADS * PEER_HALF), lambda i: (0, 0)),
                  pl.BlockSpec((2 * PEER_HEADS, PEER_KEYS, PEER_HALF), lambda i: (0, 0, 0))],
        out_specs=[pl.BlockSpec((PEER_HEADS, PEER_KEYS, tn), tok)] * 4,
        out_shape=[tab(F32), tab(F32), tab(BF16), tab(BF16)],
        scratch_shapes=[pltpu.VMEM((2 * PEER_HEADS, PEER_KEYS, tn), F32),
                        pltpu.VMEM((2 * PEER_HEADS, PEER_TOPK, tn), F32),
                        pltpu.VMEM((2 * PEER_HEADS, PEER_KEYS, tn), F32)],
        compiler_params=pltpu.CompilerParams(dimension_semantics=("arbitrary",),
                                             vmem_limit_bytes=VMEM_LIMIT_BYTES),
        name="route",
    )(h2, wq, sk)


def _peer_kernel(h2_ref, u_ref, vt_ref, e1_ref, jr_ref, e2_ref, rk2_ref, hs_ref, o_ref, acc_ref, *, rows_per_step):
    c = pl.program_id(1)
    tn = h2_ref.shape[0]

    @pl.when(c == 0)
    def _():
        acc_ref[...] = jnp.zeros_like(acc_ref)

    s_t = _nt_dot(u_ref[...], h2_ref[...])
    act = (s_t * (1.0 + lax.erf(s_t * (2.0 ** -0.5)))).astype(BF16)
    n_halves = 2 if tn % 256 == 0 else 1
    width = tn // n_halves
    shape = (PEER_KEYS, width)
    rows16 = lambda row: jnp.concatenate([jnp.broadcast_to(row, (16, width)).astype(BF16)] * (PEER_KEYS // 16), axis=0)
    halves = []
    for hb in range(n_halves):
        lanes = slice(hb * width, (hb + 1) * width)
        ws = []
        for rr in range(rows_per_step):
            gate = None
            for hd in range(PEER_HEADS):
                term = jnp.where(rk2_ref[hd, :, lanes] < rows16(jr_ref[hd, rr:rr + 1, lanes]),
                                 e2_ref[hd, :, lanes] * rows16(e1_ref[hd, rr:rr + 1, lanes]), jnp.zeros(shape, BF16))
                gate = term if gate is None else gate + term
            ws.append(act[rr * PEER_KEYS:(rr + 1) * PEER_KEYS, lanes] * gate)
        halves.append(jnp.concatenate(ws, axis=0))
    w_t = jnp.concatenate(halves, axis=1)
    acc_ref[...] += jnp.dot(vt_ref[...], w_t, preferred_element_type=F32)

    @pl.when(c == pl.num_programs(1) - 1)
    def _():
        o_ref[...] = hs_ref[...] + acc_ref[...].T


def _peer(h2, u, vt, e1, jr, e2, rk2, hs):
    n = h2.shape[0]
    tn = _largest_divisor(n, 1024, 128)
    rows_per_step = 8
    ec = rows_per_step * PEER_KEYS
    n_chunks = PEER_KEYS // rows_per_step
    tab = pl.BlockSpec((PEER_HEADS, PEER_KEYS, tn), lambda i, c: (0, 0, i))
    row_tab = pl.BlockSpec((PEER_HEADS, rows_per_step, tn), lambda i, c: (0, c, i))
    return pl.pallas_call(
        functools.partial(_peer_kernel, rows_per_step=rows_per_step),
        grid=(n // tn, n_chunks),
        in_specs=[pl.BlockSpec((tn, D_MODEL), lambda i, c: (i, 0)),
                  pl.BlockSpec((ec, D_MODEL), lambda i, c: (c, 0)),
                  pl.BlockSpec((D_MODEL, ec), lambda i, c: (0, c)),
                  row_tab, row_tab, tab, tab,
                  pl.BlockSpec((tn, D_MODEL), lambda i, c: (i, 0))],
        out_specs=pl.BlockSpec((tn, D_MODEL), lambda i, c: (i, 0)),
        out_shape=jax.ShapeDtypeStruct((n, D_MODEL), F32),
        scratch_shapes=[pltpu.VMEM((D_MODEL, tn), F32)],
        compiler_params=pltpu.CompilerParams(dimension_semantics=("arbitrary", "arbitrary"),
                                             vmem_limit_bytes=VMEM_LIMIT_BYTES),
        name="peer",
    )(h2, u, vt, e1, jr, e2, rk2, hs)


def _relayout_w_in(w):
    hq = lambda lo: (w[:, lo:lo + 512].reshape(D_MODEL, 2, DIFF_HEADS, HEAD_DIM)
                     .transpose(0, 2, 1, 3).reshape(D_MODEL, 512))
    ff = jnp.pad(w[:, 3072:3080], ((0, 0), (0, 120)))
    return jnp.concatenate([hq(0), hq(512), w[:, 1024:3072], ff, w[:, 3080:5128]], axis=1).astype(BF16)


def kernel(x, meta_tokens, norm_mix_g, w_in, b_gate, b_forget, diff_qnorm_g, diff_knorm_g, diff_lambda,
           diff_subln_g, fox_qnorm_g, fox_knorm_g, w_branch_diff, w_branch_fox, w_out, norm_ffn_g, peer_w_q,
           peer_subkeys, peer_u, peer_v):
    b, s_real, d = x.shape
    assert d == D_MODEL and s_real % 128 == 0 and norm_mix_g.shape[0] == 1
    layer = 0
    lam_init = 0.8 - 0.6 * math.exp(-0.3 * layer)

    meta_blk = jnp.pad(meta_tokens.astype(F32), ((0, META_ROWS - N_META), (0, 0)))

    def rope_tables(pos):
        inv_freq = ROPE_THETA ** (-jnp.arange(0, HEAD_DIM, 2, dtype=F32) / HEAD_DIM)
        ang = pos.astype(F32)[:, None] * inv_freq[None, :]
        return jnp.tile(jnp.cos(ang), (1, 4)), jnp.tile(jnp.sin(ang), (1, 4))

    tile8 = lambda g: jnp.tile(g.astype(F32), 8)[None, :]
    gidx = jnp.arange(BRANCH_WIDTH) // HEAD_DIM
    gmat = (gidx[:, None] == gidx[None, :]).astype(BF16)
    bf = jnp.pad(b_forget[layer].astype(F32), (0, 120))[None, :]
    w_all = _relayout_w_in(w_in[layer])

    def project(rows, pos, n_valid):
        cos, sin = rope_tables(pos)
        return _proj(rows, norm_mix_g[layer][None, :], w_all, cos, sin, tile8(diff_qnorm_g[layer]),
                     tile8(diff_knorm_g[layer]), tile8(fox_qnorm_g[layer]), tile8(fox_knorm_g[layer]),
                     bf, b_gate[layer][None, :], gmat, t_int=pos.shape[0], n_valid=n_valid)

    dq, dk, dv, fq, fk, fv, lf, ga, gb = project(x.reshape(b * s_real, d), N_META + jnp.arange(s_real), s_real)
    _, dk_m, dv_m, _, fk_m, fv_m, lf_m, _, _ = project(meta_blk, jnp.arange(META_ROWS), N_META)

    r3 = lambda a: a.reshape(b, s_real, a.shape[-1])
    tri = (jnp.arange(128)[:, None] >= jnp.arange(128)[None, :]).astype(BF16)
    cum_col, cum_row, cum_row_meta = _cum(r3(lf), lf_m, tri)

    ya = _diff_attention(_score_bound(diff_qnorm_g[layer], diff_knorm_g[layer]), r3(dq), r3(dk), r3(dv), dk_m, dv_m,
                         diff_lambda[layer].astype(F32), diff_subln_g[layer].astype(F32)[None, :], lam_init=lam_init)
    yb = _fox_attention(_score_bound(fox_qnorm_g[layer], fox_knorm_g[layer]), r3(fq), r3(fk), r3(fv), fk_m, fv_m,
                        cum_col, cum_row, cum_row_meta)

    hs, h2 = _merge(ya, yb, r3(ga), r3(gb), x, w_branch_diff[layer].astype(BF16), w_branch_fox[layer].astype(BF16),
                    w_out[layer].astype(BF16), norm_ffn_g[layer][None, :])

    n = b * s_real
    h2 = h2.reshape(n, d)
    sk = peer_subkeys[layer].astype(BF16).reshape(2 * PEER_HEADS, PEER_KEYS, PEER_HALF)
    e1, jr, e2, rk2 = _route(h2, peer_w_q[layer].astype(BF16), sk)
    half_vt = (0.5 * peer_v[layer]).astype(BF16).T
    out = _peer(h2, peer_u[layer].astype(BF16), half_vt, e1, jr, e2, rk2, hs.reshape(n, d))
    return out.reshape(b, s_real, d)
```
